```python
import math
import jax
import jax.numpy as jnp
from jax import lax
import numpy as np

D_MODEL = 4096
BATCH = 2
SEQ = 8192
DEPTH = 4
DEC_BATCH = 16
DEC_SEQ = 32
PAST_LEN = 2048

CHUNK = 64
N_META = 16
Q_BLOCK = 128
EPS = 1e-6

SB_HEAD_DIM = 128
SB_WIDTH = D_MODEL // 4
SB_HEADS = SB_WIDTH // SB_HEAD_DIM
SB_SCALE = 1.0 / math.sqrt(SB_HEAD_DIM)

POOL_WINDOWS = (2, 4, 8, 16)
POOL_GROUPS = len(POOL_WINDOWS)
POOL_WIDTH = D_MODEL // 4
POOL_GROUP_DIM = POOL_WIDTH // POOL_GROUPS
POOL_HIST = max(POOL_WINDOWS) - 1

HG_DK = 128
HG_DV = 128
HG_WIDTH = D_MODEL // 2
HG_HEADS = HG_WIDTH // HG_DK

MIX_WIDTH = SB_WIDTH + POOL_WIDTH + HG_WIDTH
PROJ_WIDTHS = (SB_WIDTH, SB_WIDTH, SB_WIDTH, POOL_WIDTH, HG_WIDTH, HG_WIDTH, HG_WIDTH, HG_WIDTH)
IN_WIDTH = sum(PROJ_WIDTHS)

D_FF = 256 * ((8 * D_MODEL // 3 + 255) // 256)
CONV_W = 3

kernel_name = 'stick_pool_hgrn_stream_encoder'


def rmsnorm(x, g):
    xf = x.astype(jnp.float32)
    y = xf * lax.rsqrt(jnp.mean(xf * xf, axis=-1, keepdims=True) + EPS)
    return (y * g.astype(jnp.float32)).astype(x.dtype)


def split_projection(p):
    offsets = np.cumsum(PROJ_WIDTHS)[:-1].tolist()
    return jnp.split(p, offsets, axis=-1)


def sb_block(q, k, v, q_pos, k_pos):
    z = jnp.einsum('bqhd,bkhd->bhqk', q, k).astype(jnp.float32) * SB_SCALE
    mask = (k_pos[None, :] < q_pos[:, None])[None, None]
    log_keep = jnp.where(mask, jax.nn.log_sigmoid(-z), 0.0)
    gap = lax.cumsum(log_keep, axis=3, reverse=True) - log_keep
    w = jnp.where(mask, jnp.exp(jax.nn.log_sigmoid(z) + gap), 0.0)
    return jnp.einsum('bhqk,bkhd->bqhd', w.astype(v.dtype), v)


def sb_prompt(q, k, v):
    b, n, h, d = q.shape
    nb = -(-n // Q_BLOCK)
    qp = jnp.pad(q, ((0, 0), (0, nb * Q_BLOCK - n), (0, 0), (0, 0)))
    q_blocks = qp.reshape(b, nb, Q_BLOCK, h, d).transpose(1, 0, 2, 3, 4)
    q_pos = jnp.arange(nb * Q_BLOCK, dtype=jnp.int32).reshape(nb, Q_BLOCK)
    k_pos = jnp.arange(n, dtype=jnp.int32)
    out = lax.map(lambda blk: sb_block(blk[0], k, v, blk[1], k_pos), (q_blocks, q_pos))
    return out.transpose(1, 0, 2, 3, 4).reshape(b, nb * Q_BLOCK, h, d)[:, :n]


def sb_sample(q, k_new, v_new, k_cache, v_cache):
    p = k_cache.shape[1]
    n = q.shape[1]
    k = jnp.concatenate([k_cache.astype(q.dtype), k_new], axis=1)
    v = jnp.concatenate([v_cache.astype(q.dtype), v_new], axis=1)
    k_pos = jnp.arange(p + n, dtype=jnp.int32)
    q_pos = p + jnp.arange(n, dtype=jnp.int32)
    return sb_block(q, k, v, q_pos, k_pos)


def pool_mix(xb, hist, n_hist, w_pool, scale):
    b, n, c = xb.shape
    f32 = jnp.float32
    xf = xb.astype(f32)
    ext = jnp.concatenate([hist.astype(f32), xf], axis=1)
    cs = jnp.concatenate([jnp.zeros((b, 1, c), f32), jnp.cumsum(ext, axis=1)], axis=1)
    t = jnp.arange(n)
    end = cs[:, POOL_HIST + 1:]
    groups = []
    for g, w in enumerate(POOL_WINDOWS):
        sl = slice(g * POOL_GROUP_DIM, (g + 1) * POOL_GROUP_DIM)
        start = cs[:, POOL_HIST + 1 - w: POOL_HIST + 1 - w + n, sl]
        count = jnp.minimum(t + 1 + n_hist, w).astype(f32)[None, :, None]
        groups.append((end[..., sl] - start) / count - xf[..., sl])
    pooled = jnp.stack(groups, axis=2)
    y = jnp.einsum('bngc,gcd->bngd', pooled, w_pool.astype(f32)).reshape(b, n, c)
    y = y * scale.astype(f32)
    return y.astype(xb.dtype), ext[:, -POOL_HIST:].astype(xb.dtype)


def hgrn_mix(qc, fc, ic, gc, lb, s0, norm_g):
    b, n, _ = qc.shape
    f32 = jnp.float32
    L = CHUNK if n > CHUNK else n
    pad = (-n) % L
    nc = (n + pad) // L
    q = jax.nn.silu(qc.astype(f32))
    log_f = jnp.logaddexp(jnp.log(lb), jnp.log1p(-lb) + jax.nn.log_sigmoid(fc.astype(f32)))
    k = -jnp.expm1(log_f)
    v = ic.astype(f32)

    def blocks(a, d):
        a = jnp.pad(a.reshape(b, n, HG_HEADS, d), ((0, 0), (pad, 0), (0, 0), (0, 0)))
        return a.reshape(b, nc, L, HG_HEADS, d).transpose(1, 0, 2, 3, 4)

    xs = (blocks(q, HG_DK), blocks(k, HG_DK), blocks(log_f, HG_DK), blocks(v, HG_DV))
    causal = jnp.tril(jnp.ones((L, L), dtype=bool))[None, :, :, None, None]

    def step(S, blk):
        qb, kb, gb, vb = blk
        cum = jnp.cumsum(gb, axis=1)
        decay = jnp.exp(jnp.where(causal, cum[:, :, None] - cum[:, None, :], -jnp.inf))
        scores = jnp.sum(qb[:, :, None] * kb[:, None, :] * decay, axis=-1)
        o = (jnp.einsum('btsh,bshv->bthv', scores, vb)
             + jnp.einsum('bthk,bhkv->bthv', qb * jnp.exp(cum), S))
        last = cum[:, -1]
        S_new = (jnp.exp(last)[..., None] * S
                 + jnp.einsum('bshk,bshv->bhkv', kb * jnp.exp(last[:, None] - cum), vb))
        return S_new, o

    s_final, o = lax.scan(step, s0.astype(f32), xs)
    o = o.transpose(1, 0, 2, 3, 4).reshape(b, nc * L, HG_HEADS, HG_DV)[:, pad:]
    o = o * lax.rsqrt(jnp.mean(o * o, axis=-1, keepdims=True) + EPS) * norm_g.astype(f32)
    o = o.reshape(b, n, HG_WIDTH) * jax.nn.silu(gc.astype(f32))
    return o.astype(qc.dtype), s_final


def token_mixers(h, w_in_l, pool_w_l, pool_scale_l, lb_l, hg_g_l, w_out_l,
                 sb_cache, pool_hist, n_hist, s0):
    b, n, _ = h.shape
    qa, ka, va, xb, qc, fc, ic, gc = split_projection(h @ w_in_l)
    qa = qa.reshape(b, n, SB_HEADS, SB_HEAD_DIM)
    ka = ka.reshape(b, n, SB_HEADS, SB_HEAD_DIM)
    va = va.reshape(b, n, SB_HEADS, SB_HEAD_DIM)
    if sb_cache is None:
        a_out = sb_prompt(qa, ka, va)
    else:
        a_out = sb_sample(qa, ka, va, sb_cache[0], sb_cache[1])
    b_out, pool_new = pool_mix(xb, pool_hist, n_hist, pool_w_l, pool_scale_l)
    c_out, s_new = hgrn_mix(qc, fc, ic, gc, lb_l, s0, hg_g_l)
    mixed = jnp.concatenate([a_out.reshape(b, n, SB_WIDTH), b_out, c_out], axis=-1) @ w_out_l
    return mixed, ka, va, s_new, pool_new


def conv_ffn(h, hist, w_gate, w_up, conv_w, conv_b, w_down):
    n = h.shape[1]
    g = h @ w_gate
    u = h @ w_up
    ext = jnp.concatenate([hist.astype(g.dtype), g], axis=1)
    conv = conv_b + ext[:, 0:n] * conv_w[0]
    for j in range(1, CONV_W):
        conv = conv + ext[:, j:j + n] * conv_w[j]
    y = (jax.nn.silu(conv) * u) @ w_down
    return y, ext[:, -(CONV_W - 1):]


def setup_inputs(seed: int = 0) -> dict:
    key = jax.random.key(seed)
    ks = jax.random.split(key, 24)
    f32 = jnp.float32

    def nrm(k, shape, s):
        return jax.random.normal(k, shape, f32) * s

    return {
        'x_prompt': nrm(ks[0], (BATCH, SEQ, D_MODEL), 1.0),
        'x_sample': nrm(ks[1], (DEC_BATCH, DEC_SEQ, D_MODEL), 1.0),
        'cache_sb_k': nrm(ks[2], (DEPTH, DEC_BATCH, PAST_LEN, SB_HEADS, SB_HEAD_DIM), 1.0),
        'cache_sb_v': nrm(ks[3], (DEPTH, DEC_BATCH, PAST_LEN, SB_HEADS, SB_HEAD_DIM), 1.0),
        'state_hgrn': nrm(ks[4], (DEPTH, DEC_BATCH, HG_HEADS, HG_DK, HG_DV), 0.5),
        'state_pool': nrm(ks[5], (DEPTH, DEC_BATCH, POOL_HIST, POOL_WIDTH), 1.0),
        'state_conv': nrm(ks[6], (DEPTH, DEC_BATCH, CONV_W - 1, D_FF), 1.0),
        'meta_tokens': nrm(ks[7], (N_META, D_MODEL), 1.0),
        'norm1_g': 1.0 + nrm(ks[8], (DEPTH, D_MODEL), 0.02),
        'w_in': nrm(ks[9], (DEPTH, D_MODEL, IN_WIDTH), D_MODEL ** -0.5),
        'pool_w': nrm(ks[10], (DEPTH, POOL_GROUPS, POOL_GROUP_DIM, POOL_GROUP_DIM), POOL_GROUP_DIM ** -0.5),
        'pool_scale': 1.0 + nrm(ks[11], (DEPTH, POOL_WIDTH), 0.02),
        'hgrn_lower_bounds': nrm(ks[12], (DEPTH, HG_WIDTH), 1.0),
        'hgrn_norm_g': 1.0 + nrm(ks[13], (DEPTH, HG_DV), 0.02),
        'w_out': nrm(ks[14], (DEPTH, MIX_WIDTH, D_MODEL), MIX_WIDTH ** -0.5),
        'norm2_g': 1.0 + nrm(ks[15], (DEPTH, D_MODEL), 0.02),
        'ffn_w_gate': nrm(ks[16], (DEPTH, D_MODEL, D_FF), D_MODEL ** -0.5),
        'ffn_w_up': nrm(ks[17], (DEPTH, D_MODEL, D_FF), D_MODEL ** -0.5),
        'ffn_conv_w': nrm(ks[18], (DEPTH, CONV_W, D_FF), CONV_W ** -0.5),
        'ffn_conv_b': nrm(ks[19], (DEPTH, D_FF), 0.01),
        'ffn_w_down': nrm(ks[20], (DEPTH, D_FF, D_MODEL), D_FF ** -0.5),
        'final_norm_g': 1.0 + nrm(ks[21], (D_MODEL,), 0.02),
    }


def reference(x_prompt, x_sample, cache_sb_k, cache_sb_v, state_hgrn, state_pool, state_conv,
              meta_tokens, norm1_g, w_in, pool_w, pool_scale, hgrn_lower_bounds, hgrn_norm_g,
              w_out, norm2_g, ffn_w_gate, ffn_w_up, ffn_conv_w, ffn_conv_b, ffn_w_down,
              final_norm_g):
    dt = x_prompt.dtype
    bp = x_prompt.shape[0]
    meta = jnp.broadcast_to(meta_tokens.astype(dt)[None], (bp, N_META, D_MODEL))
    xp = jnp.concatenate([meta, x_prompt], axis=1)
    xs = x_sample
    probs = jax.nn.softmax(hgrn_lower_bounds.astype(jnp.float32), axis=0)
    lower = jnp.maximum(jnp.cumsum(probs, axis=0) - probs[0], 0.0)

    pk, pv, ps, pp, pc = [], [], [], [], []
    sk, sv, ss, sp, sc = [], [], [], [], []
    for l in range(DEPTH):
        mixed, k_l, v_l, s_l, pool_l = token_mixers(
            rmsnorm(xp, norm1_g[l]), w_in[l], pool_w[l], pool_scale[l], lower[l], hgrn_norm_g[l], w_out[l],
            None, jnp.zeros((bp, POOL_HIST, POOL_WIDTH), dt), 0,
            jnp.zeros((bp, HG_HEADS, HG_DK, HG_DV), jnp.float32))
        xp = xp + mixed
        f_l, conv_l = conv_ffn(rmsnorm(xp, norm2_g[l]), jnp.zeros((bp, CONV_W - 1, D_FF), dt),
                               ffn_w_gate[l], ffn_w_up[l], ffn_conv_w[l], ffn_conv_b[l], ffn_w_down[l])
        xp = xp + f_l
        pk.append(k_l)
        pv.append(v_l)
        ps.append(s_l.astype(state_hgrn.dtype))
        pp.append(pool_l)
        pc.append(conv_l)

        mixed, k_l, v_l, s_l, pool_l = token_mixers(
            rmsnorm(xs, norm1_g[l]), w_in[l], pool_w[l], pool_scale[l], lower[l], hgrn_norm_g[l], w_out[l],
            (cache_sb_k[l], cache_sb_v[l]), state_pool[l], POOL_HIST, state_hgrn[l])
        xs = xs + mixed
        f_l, conv_l = conv_ffn(rmsnorm(xs, norm2_g[l]), state_conv[l],
                               ffn_w_gate[l], ffn_w_up[l], ffn_conv_w[l], ffn_conv_b[l], ffn_w_down[l])
        xs = xs + f_l
        sk.append(k_l)
        sv.append(v_l)
        ss.append(s_l.astype(state_hgrn.dtype))
        sp.append(pool_l)
        sc.append(conv_l)

    y_prompt = rmsnorm(xp, final_norm_g)[:, N_META:]
    y_sample = rmsnorm(xs, final_norm_g)
    return (y_prompt, y_sample,
            jnp.stack(pk), jnp.stack(pv), jnp.stack(ps), jnp.stack(pp), jnp.stack(pc),
            jnp.stack(sk), jnp.stack(sv), jnp.stack(ss), jnp.stack(sp), jnp.stack(sc))
```

```python
import functools
import math

import jax
import jax.numpy as jnp
from jax import lax
from jax.experimental import pallas as pl
from jax.experimental.pallas import tpu as pltpu

F32 = jnp.float32
BF16 = jnp.bfloat16

N_META = 16
EPS = 1e-6
HEAD_DIM = 128
POOL_WINDOWS = (2, 4, 8, 16)
POOL_HIST = max(POOL_WINDOWS) - 1
CONV_W = 3

VMEM_LIMIT_BYTES = 56 * 1024 * 1024
BF16_SUBLANES = 16
MAX_ROW_TILE = 1024
MAX_NORM_ROWS = 512
MAX_HGRN_ROWS = 256
SB_BLOCK = 256
HG_CHUNK = 16
COL_TILE = 512
LANE = 128


def _row_tile(rows, limit):
    best = None
    for t in range(BF16_SUBLANES, min(rows, limit) + 1, BF16_SUBLANES):
        if rows % t == 0:
            best = t
    assert best is not None, rows
    return best


def _params(*semantics):
    return pltpu.CompilerParams(dimension_semantics=semantics, vmem_limit_bytes=VMEM_LIMIT_BYTES)


def _sigmoid(x):
    return 1.0 / (1.0 + jnp.exp(-x))


def _rmsnorm_kernel(x_ref, g_ref, o_ref):
    x = x_ref[...]
    ms = jnp.mean(x * x, axis=-1, keepdims=True)
    o_ref[...] = (x * lax.rsqrt(ms + EPS) * g_ref[...]).astype(o_ref.dtype)


def _rmsnorm(x, g, out_dtype):
    rows, d = x.shape
    tr = _row_tile(rows, MAX_NORM_ROWS)
    return pl.pallas_call(
        _rmsnorm_kernel,
        out_shape=jax.ShapeDtypeStruct((rows, d), out_dtype),
        grid=(rows // tr,),
        in_specs=[pl.BlockSpec((tr, d), lambda i: (i, 0)),
                  pl.BlockSpec((1, d), lambda i: (0, 0))],
        out_specs=pl.BlockSpec((tr, d), lambda i: (i, 0)),
        compiler_params=_params("parallel"),
        name="rmsnorm",
    )(x, g.reshape(1, d).astype(F32))


def _matmul_kernel(x_ref, w_ref, o_ref):
    o_ref[...] = jnp.dot(x_ref[...], w_ref[...], preferred_element_type=F32)


def _matmul(x, w, tm):
    rows, k = x.shape
    n = w.shape[1]
    tn = min(2 * COL_TILE, n)
    return pl.pallas_call(
        _matmul_kernel,
        out_shape=jax.ShapeDtypeStruct((rows, n), F32),
        grid=(pl.cdiv(n, tn), rows // tm),
        in_specs=[pl.BlockSpec((tm, k), lambda j, i: (i, 0)),
                  pl.BlockSpec((k, tn), lambda j, i: (0, j))],
        out_specs=pl.BlockSpec((tm, tn), lambda j, i: (i, j)),
        compiler_params=_params("parallel", "parallel"),
        name="matmul",
    )(x, w)


def _outproj_kernel(a_ref, b_ref, c_ref, wa_ref, wb_ref, wc_ref, x_ref, o_ref):
    acc = jnp.dot(a_ref[...], wa_ref[...], preferred_element_type=F32)
    acc += jnp.dot(b_ref[...], wb_ref[...], preferred_element_type=F32)
    acc += jnp.dot(c_ref[...], wc_ref[...], preferred_element_type=F32)
    o_ref[...] = x_ref[...] + acc


def _outproj(a, b, c, w_out, x, tm):
    rows, d = x.shape
    wa, wb, wc = a.shape[1], b.shape[1], c.shape[1]
    assert wa == wb and wc == 2 * wa and w_out.shape[0] == wa + wb + wc
    tn = min(COL_TILE, d)
    return pl.pallas_call(
        _outproj_kernel,
        out_shape=jax.ShapeDtypeStruct((rows, d), F32),
        grid=(d // tn, rows // tm),
        in_specs=[pl.BlockSpec((tm, wa), lambda j, i: (i, 0)),
                  pl.BlockSpec((tm, wb), lambda j, i: (i, 0)),
                  pl.BlockSpec((tm, wc), lambda j, i: (i, 0)),
                  pl.BlockSpec((wa, tn), lambda j, i: (0, j)),
                  pl.BlockSpec((wb, tn), lambda j, i: (1, j)),
                  pl.BlockSpec((wc, tn), lambda j, i: (1, j)),
                  pl.BlockSpec((tm, tn), lambda j, i: (i, j))],
        out_specs=pl.BlockSpec((tm, tn), lambda j, i: (i, j)),
        compiler_params=_params("parallel", "parallel"),
        name="outproj",
    )(a, b, c, w_out, w_out, w_out, x)


def _ffn_act(g, g1, g2, u, cw_ref, cb_ref):
    conv = cb_ref[...] + g2 * cw_ref[0:1, :] + g1 * cw_ref[1:2, :] + g * cw_ref[2:3, :]
    return (conv * _sigmoid(conv) * u).astype(BF16)


def _gateup_carry_kernel(h_ref, wg_ref, wu_ref, cw_ref, cb_ref, o_ref, carry_ref, *, tiles_per_seq):
    i = pl.program_id(1)
    h = h_ref[...]
    g = jnp.dot(h, wg_ref[...], preferred_element_type=F32)
    u = jnp.dot(h, wu_ref[...], preferred_element_type=F32)
    tm = g.shape[0]

    @pl.when(i % tiles_per_seq == 0)
    def _():
        carry_ref[...] = jnp.zeros_like(carry_ref)

    prev = carry_ref[...]
    row = lax.broadcasted_iota(jnp.int32, g.shape, 0)
    g1 = jnp.where(row == 0, prev[7:8, :], pltpu.roll(g, 1, 0))
    g2 = jnp.where(row == 0, prev[6:7, :], jnp.where(row == 1, prev[7:8, :], pltpu.roll(g, 2, 0)))
    carry_ref[...] = g[tm - 8:tm, :]
    o_ref[...] = _ffn_act(g, g1, g2, u, cw_ref, cb_ref)


def _gateup_hist_kernel(h_ref, wg_ref, wu_ref, cw_ref, cb_ref, h1_ref, h2_ref, o_ref, *, seq_len):
    h = h_ref[...]
    g = jnp.dot(h, wg_ref[...], preferred_element_type=F32)
    u = jnp.dot(h, wu_ref[...], preferred_element_type=F32)
    pos = lax.broadcasted_iota(jnp.int32, g.shape, 0) % seq_len
    g1 = jnp.where(pos == 0, h1_ref[...], pltpu.roll(g, 1, 0))
    g2 = jnp.where(pos < 2, h2_ref[...], pltpu.roll(g, 2, 0))
    o_ref[...] = _ffn_act(g, g1, g2, u, cw_ref, cb_ref)


def _gateup(h, w_gate, w_up, conv_w, conv_b, tm, seq_len, hist):
    rows, d = h.shape
    n = w_gate.shape[1]
    tn = min(COL_TILE, n)
    grid = (pl.cdiv(n, tn), rows // tm)
    row_spec = pl.BlockSpec((tm, d), lambda j, i: (i, 0))
    w_spec = pl.BlockSpec((d, tn), lambda j, i: (0, j))
    cw_spec = pl.BlockSpec((CONV_W, tn), lambda j, i: (0, j))
    cb_spec = pl.BlockSpec((1, tn), lambda j, i: (0, j))
    out_spec = pl.BlockSpec((tm, tn), lambda j, i: (i, j))
    out_shape = jax.ShapeDtypeStruct((rows, n), BF16)
    cb = conv_b.reshape(1, n)
    if hist is None:
        assert seq_len % tm == 0
        return pl.pallas_call(
            functools.partial(_gateup_carry_kernel, tiles_per_seq=seq_len // tm),
            out_shape=out_shape, grid=grid,
            in_specs=[row_spec, w_spec, w_spec, cw_spec, cb_spec],
            out_specs=out_spec,
            scratch_shapes=[pltpu.VMEM((8, tn), F32)],
            compiler_params=_params("arbitrary", "arbitrary"),
            name="gateup_carry",
        )(h, w_gate, w_up, conv_w, cb)
    assert tm % seq_len == 0
    nb = rows // seq_len
    zeros = jnp.zeros((nb, seq_len, n), F32)
    h1 = zeros.at[:, 0].set(hist[:, 1]).reshape(rows, n)
    h2 = zeros.at[:, 0].set(hist[:, 0]).at[:, 1].set(hist[:, 1]).reshape(rows, n)
    hist_spec = pl.BlockSpec((tm, tn), lambda j, i: (i, j))
    return pl.pallas_call(
        functools.partial(_gateup_hist_kernel, seq_len=seq_len),
        out_shape=out_shape, grid=grid,
        in_specs=[row_spec, w_spec, w_spec, cw_spec, cb_spec, hist_spec, hist_spec],
        out_specs=out_spec,
        compiler_params=_params("parallel", "parallel"),
        name="gateup_hist",
    )(h, w_gate, w_up, conv_w, cb, h1, h2)


def _down_kernel(a_ref, w_ref, x_ref, o_ref):
    k = pl.program_id(2)
    part = jnp.dot(a_ref[...], w_ref[...], preferred_element_type=F32)

    @pl.when(k == 0)
    def _():
        o_ref[...] = x_ref[...] + part

    @pl.when(k != 0)
    def _():
        o_ref[...] += part


def _down(act, w_down, x, tm):
    rows, n = act.shape
    d = x.shape[1]
    tn = min(COL_TILE, d)
    tk = n // 2 if (n % (2 * LANE) == 0) else n
    return pl.pallas_call(
        _down_kernel,
        out_shape=jax.ShapeDtypeStruct((rows, d), F32),
        grid=(rows // tm, d // tn, n // tk),
        in_specs=[pl.BlockSpec((tm, tk), lambda i, j, k: (i, k)),
                  pl.BlockSpec((tk, tn), lambda i, j, k: (k, j)),
                  pl.BlockSpec((tm, tn), lambda i, j, k: (i, j))],
        out_specs=pl.BlockSpec((tm, tn), lambda i, j, k: (i, j)),
        compiler_params=_params("parallel", "parallel", "arbitrary"),
        name="down",
    )(act, w_down, x)


def _sb_block(q, kblk, vblk, u, carry, acc, mask):
    z = lax.dot_general(q, kblk, (((1,), (1,)), ((), ())), preferred_element_type=F32)
    sp = jnp.maximum(z, 0.0) + jnp.log(1.0 + jnp.exp(-jnp.abs(z)))
    log_keep = -sp
    log_beta = z - sp
    if mask is not None:
        log_keep = jnp.where(mask, log_keep, 0.0)
    hi = log_keep.astype(BF16)
    rem = log_keep - hi.astype(F32)
    mid = rem.astype(BF16)
    lo = (rem - mid.astype(F32)).astype(BF16)
    gap = (jnp.dot(hi, u, preferred_element_type=F32) + jnp.dot(mid, u, preferred_element_type=F32)
           + jnp.dot(lo, u, preferred_element_type=F32)) + carry
    w = jnp.exp(log_beta + gap)
    if mask is not None:
        w = jnp.where(mask, w, 0.0)
    acc = acc + jnp.dot(w.astype(BF16), vblk, preferred_element_type=F32)
    carry = carry + jnp.sum(log_keep, axis=-1, keepdims=True)
    return carry, acc


def _sb_prompt_kernel(q_ref, k_ref, v_ref, u_ref, o_ref, kb_ref, vb_ref, *, scale, t_len):
    qi = pl.program_id(2)
    blk = SB_BLOCK
    t_pad = kb_ref.shape[0]

    @pl.when(qi == 0)
    def _():
        kb_ref[0:t_len, :] = k_ref[0].astype(BF16)
        vb_ref[0:t_len, :] = v_ref[0].astype(BF16)
        if t_pad > t_len:
            kb_ref[t_len:t_pad, :] = jnp.zeros((t_pad - t_len, HEAD_DIM), BF16)
            vb_ref[t_len:t_pad, :] = jnp.zeros((t_pad - t_len, HEAD_DIM), BF16)

    q = (q_ref[0] * scale).astype(BF16)
    u = u_ref[...]
    row = lax.broadcasted_iota(jnp.int32, (blk, blk), 0)
    col = lax.broadcasted_iota(jnp.int32, (blk, blk), 1)
    start = pl.multiple_of(qi * blk, blk)
    carry = jnp.zeros((blk, 1), F32)
    acc = jnp.zeros((blk, HEAD_DIM), F32)
    carry, acc = _sb_block(q, kb_ref[pl.ds(start, blk), :], vb_ref[pl.ds(start, blk), :], u,
                           carry, acc, col < row)

    def body(it, state):
        s = pl.multiple_of((qi - 1 - it) * blk, blk)
        return _sb_block(q, kb_ref[pl.ds(s, blk), :], vb_ref[pl.ds(s, blk), :], u, state[0], state[1], None)

    carry, acc = lax.fori_loop(0, qi, body, (carry, acc))
    o_ref[0] = acc.astype(o_ref.dtype)


def _tri_ones(n):
    r = lax.broadcasted_iota(jnp.int32, (n, n), 0)
    c = lax.broadcasted_iota(jnp.int32, (n, n), 1)
    return (r > c).astype(BF16)


def _sb_prompt(proj, n_heads):
    b, t, _ = proj.shape
    blk = SB_BLOCK
    nq = pl.cdiv(t, blk)
    t_pad = nq * blk
    scale = 1.0 / math.sqrt(HEAD_DIM)
    return pl.pallas_call(
        functools.partial(_sb_prompt_kernel, scale=scale, t_len=t),
        out_shape=jax.ShapeDtypeStruct((b, t, n_heads * HEAD_DIM), BF16),
        grid=(b, n_heads, nq),
        in_specs=[pl.BlockSpec((1, blk, HEAD_DIM), lambda bi, h, qi: (bi, qi, h)),
                  pl.BlockSpec((1, t, HEAD_DIM), lambda bi, h, qi: (bi, 0, n_heads + h)),
                  pl.BlockSpec((1, t, HEAD_DIM), lambda bi, h, qi: (bi, 0, 2 * n_heads + h)),
                  pl.BlockSpec((blk, blk), lambda bi, h, qi: (0, 0))],
        out_specs=pl.BlockSpec((1, blk, HEAD_DIM), lambda bi, h, qi: (bi, qi, h)),
        scratch_shapes=[pltpu.VMEM((t_pad, HEAD_DIM), BF16), pltpu.VMEM((t_pad, HEAD_DIM), BF16)],
        compiler_params=_params("parallel", "parallel", "arbitrary"),
        name="sb_prompt",
    )(proj, proj, proj, _tri_ones(blk))


def _sb_sample_kernel(q_ref, kn_ref, vn_ref, kc_ref, vc_ref, u_ref, o_ref, *, scale, n_cache_blocks):
    blk = SB_BLOCK
    n = q_ref.shape[1]
    q = (q_ref[0] * scale).astype(BF16)
    u = u_ref[...]
    row = lax.broadcasted_iota(jnp.int32, (n, n), 0)
    col = lax.broadcasted_iota(jnp.int32, (n, n), 1)
    carry = jnp.zeros((n, 1), F32)
    acc = jnp.zeros((n, HEAD_DIM), F32)
    carry, acc = _sb_block(q, kn_ref[0].astype(BF16), vn_ref[0].astype(BF16), u[0:n, 0:n],
                           carry, acc, col < row)

    for j in reversed(range(n_cache_blocks)):
        kblk = kc_ref[0, j * blk:(j + 1) * blk, :].astype(BF16)
        vblk = vc_ref[0, j * blk:(j + 1) * blk, :].astype(BF16)
        carry, acc = _sb_block(q, kblk, vblk, u, carry, acc, None)
    o_ref[0] = acc.astype(o_ref.dtype)


def _sb_sample(proj, cache_k, cache_v, n_heads):
    b, n, _ = proj.shape
    p = cache_k.shape[1]
    blk = SB_BLOCK
    assert p % blk == 0 and n <= blk
    scale = 1.0 / math.sqrt(HEAD_DIM)
    new_spec = lambda off: pl.BlockSpec((1, n, HEAD_DIM), lambda bi, h: (bi, 0, off + h))
    cache_spec = pl.BlockSpec((1, p, HEAD_DIM), lambda bi, h: (bi, 0, h))
    return pl.pallas_call(
        functools.partial(_sb_sample_kernel, scale=scale, n_cache_blocks=p // blk),
        out_shape=jax.ShapeDtypeStruct((b, n, n_heads * HEAD_DIM), BF16),
        grid=(b, n_heads),
        in_specs=[new_spec(0), new_spec(n_heads), new_spec(2 * n_heads), cache_spec, cache_spec,
                  pl.BlockSpec((blk, blk), lambda bi, h: (0, 0))],
        out_specs=pl.BlockSpec((1, n, HEAD_DIM), lambda bi, h: (bi, 0, h)),
        compiler_params=_params("parallel", "parallel"),
        name="sb_sample",
    )(proj, proj, proj, cache_k, cache_v, _tri_ones(blk))


def _pool_kernel(x_ref, hist_ref, w_ref, scale_ref, o_ref, ext_ref, *, n_hist, group_dim):
    ti = pl.program_id(1)
    tm = x_ref.shape[1]
    pad = POOL_HIST + 1

    @pl.when(ti == 0)
    def _():
        ext_ref[0:pad, :] = hist_ref[0]

    @pl.when(ti != 0)
    def _():
        ext_ref[0:pad, :] = ext_ref[tm:tm + pad, :]

    ext_ref[pad:pad + tm, :] = x_ref[0]
    t = ti * tm + lax.broadcasted_iota(jnp.int32, (tm, 1), 0)
    for gi, win in enumerate(POOL_WINDOWS):
        cols = slice(gi * group_dim, (gi + 1) * group_dim)
        x = ext_ref[pad:pad + tm, cols]
        total = x
        for d in range(1, win):
            total = total + ext_ref[pad - d:pad - d + tm, cols]
        count = jnp.minimum(t + 1 + n_hist, win).astype(F32)
        pooled = total / count - x
        y = jnp.dot(pooled.astype(BF16), w_ref[gi], preferred_element_type=F32)
        o_ref[0, :, cols] = (y * scale_ref[:, cols]).astype(o_ref.dtype)


def _pool(proj, col_block, width, hist, n_hist, w_pool, scale, tm):
    b, t, _ = proj.shape
    groups = len(POOL_WINDOWS)
    group_dim = width // groups
    pad = POOL_HIST + 1
    hist_pad = jnp.concatenate([jnp.zeros((b, 1, width), F32), hist], axis=1)
    return pl.pallas_call(
        functools.partial(_pool_kernel, n_hist=n_hist, group_dim=group_dim),
        out_shape=jax.ShapeDtypeStruct((b, t, width), BF16),
        grid=(b, t // tm),
        in_specs=[pl.BlockSpec((1, tm, width), lambda bi, ti: (bi, ti, col_block)),
                  pl.BlockSpec((1, pad, width), lambda bi, ti: (bi, 0, 0)),
                  pl.BlockSpec((groups, group_dim, group_dim), lambda bi, ti: (0, 0, 0)),
                  pl.BlockSpec((1, width), lambda bi, ti: (0, 0))],
        out_specs=pl.BlockSpec((1, tm, width), lambda bi, ti: (bi, ti, 0)),
        scratch_shapes=[pltpu.VMEM((tm + pad, width), F32)],
        compiler_params=_params("parallel", "arbitrary"),
        name="pool",
    )(proj, hist_pad, w_pool, scale.reshape(1, width))


def _cumsum_rows(x):
    row = lax.broadcasted_iota(jnp.int32, x.shape, 0)
    shift = 1
    while shift < x.shape[0]:
        x = x + jnp.where(row >= shift, pltpu.roll(x, shift, 0), 0.0)
        shift *= 2
    return x


def _hgrn_kernel(q_ref, f_ref, i_ref, g_ref, la_ref, lc_ref, ng_ref, s0_ref, o_ref, sout_ref, s_ref,
                 *, n_heads, n_chunks):
    tb = pl.program_id(1)
    half = HG_CHUNK // 2
    hd = HEAD_DIM

    @pl.when(tb == 0)
    def _():
        s_ref[...] = s0_ref[0]

    la = la_ref[...]
    lc = lc_ref[...]
    ng = ng_ref[...]
    row8 = lax.broadcasted_iota(jnp.int32, (half, hd), 0)

    def chunk(c, carry):
        r0 = pl.multiple_of(c * HG_CHUNK, HG_CHUNK)
        qc = q_ref[0, pl.ds(r0, HG_CHUNK), :]
        fc = f_ref[0, pl.ds(r0, HG_CHUNK), :]
        v = i_ref[0, pl.ds(r0, HG_CHUNK), :]
        gc = g_ref[0, pl.ds(r0, HG_CHUNK), :]
        q = qc * _sigmoid(qc)
        gate = gc * _sigmoid(gc)
        l1 = jnp.log(1.0 + jnp.exp(-jnp.abs(fc)))
        log_sig = jnp.minimum(fc, 0.0) - l1
        log_sig_neg = jnp.minimum(-fc, 0.0) - l1
        b = lc + log_sig
        log_f = jnp.maximum(la, b) + jnp.log(1.0 + jnp.exp(-jnp.abs(la - b)))
        k = jnp.exp(lc + log_sig_neg)
        cum = _cumsum_rows(log_f)
        last = cum[HG_CHUNK - 1:HG_CHUNK, :]
        qt = (q * jnp.exp(cum)).astype(BF16)
        kt = (k * jnp.exp(last - cum)).astype(BF16)
        dec = jnp.exp(last)
        vb = v.astype(BF16)
        for h in range(n_heads):
            sl = slice(h * hd, (h + 1) * hd)
            cum_h, q_h, k_h, v_h = cum[:, sl], q[:, sl], k[:, sl], v[:, sl]
            cum_a, cum_b = cum_h[0:half], cum_h[half:]
            q_a, q_b = q_h[0:half], q_h[half:]
            o_a = jnp.zeros((half, hd), F32)
            o_b = jnp.zeros((half, hd), F32)
            for s in range(HG_CHUNK):
                cs, ks, vs = cum_h[s:s + 1], k_h[s:s + 1], v_h[s:s + 1]
                if s < half:
                    e_a = jnp.where(row8 >= s, jnp.exp(cum_a - cs), 0.0)
                    o_a = o_a + jnp.sum(q_a * e_a * ks, axis=-1, keepdims=True) * vs
                    e_b = jnp.exp(cum_b - cs)
                else:
                    e_b = jnp.where(row8 + half >= s, jnp.exp(cum_b - cs), 0.0)
                o_b = o_b + jnp.sum(q_b * e_b * ks, axis=-1, keepdims=True) * vs
            st = s_ref[h]
            inter = lax.dot_general(qt[:, sl], st.astype(BF16), (((1,), (1,)), ((), ())),
                                    preferred_element_type=F32)
            o = jnp.concatenate([o_a, o_b], axis=0) + inter
            ms = jnp.mean(o * o, axis=-1, keepdims=True)
            o = o * lax.rsqrt(ms + EPS) * ng * gate[:, sl]
            o_ref[0, pl.ds(r0, HG_CHUNK), sl] = o.astype(o_ref.dtype)
            s_ref[h] = st * dec[:, sl] + lax.dot_general(vb[:, sl], kt[:, sl], (((0,), (0,)), ((), ())),
                                                         preferred_element_type=F32)
        return carry

    lax.fori_loop(0, n_chunks, chunk, 0)

    @pl.when(tb == pl.num_programs(1) - 1)
    def _():
        sout_ref[0] = s_ref[...]


def _hgrn(proj, first_block, width, log_lb, log_1m_lb, norm_g, s0_t, tb_rows):
    b, t, _ = proj.shape
    n_heads = width // HEAD_DIM
    assert t % tb_rows == 0 and tb_rows % HG_CHUNK == 0
    col = lambda off: pl.BlockSpec((1, tb_rows, width), lambda bi, ti: (bi, ti, first_block + off))
    vec = pl.BlockSpec((1, width), lambda bi, ti: (0, 0))
    state_spec = pl.BlockSpec((1, n_heads, HEAD_DIM, HEAD_DIM), lambda bi, ti: (bi, 0, 0, 0))
    return pl.pallas_call(
        functools.partial(_hgrn_kernel, n_heads=n_heads, n_chunks=tb_rows // HG_CHUNK),
        out_shape=(jax.ShapeDtypeStruct((b, t, width), BF16),
                   jax.ShapeDtypeStruct((b, n_heads, HEAD_DIM, HEAD_DIM), F32)),
        grid=(b, t // tb_rows),
        in_specs=[col(0), col(1), col(2), col(3), vec, vec,
                  pl.BlockSpec((1, HEAD_DIM), lambda bi, ti: (0, 0)), state_spec],
        out_specs=(pl.BlockSpec((1, tb_rows, width), lambda bi, ti: (bi, ti, 0)), state_spec),
        scratch_shapes=[pltpu.VMEM((n_heads, HEAD_DIM, HEAD_DIM), F32)],
        compiler_params=_params("parallel", "arbitrary"),
        name="hgrn",
    )(proj, proj, proj, proj, log_lb.reshape(1, width), log_1m_lb.reshape(1, width),
      norm_g.reshape(1, HEAD_DIM).astype(F32), s0_t)


def _layer(x, batch, seq, lw, sb_cache, pool_hist, n_hist, s0, conv_hist):
    rows, d = x.shape
    sb_width = d // 4
    pool_width = d // 4
    hg_width = d // 2
    n_sb_heads = sb_width // HEAD_DIM
    tm = _row_tile(seq, MAX_ROW_TILE) if conv_hist is None else rows
    assert rows % tm == 0

    h = _rmsnorm(x, lw["norm1_g"], BF16)
    proj = _matmul(h, lw["w_in"], tm)
    proj3 = proj.reshape(batch, seq, 3 * d)
    if sb_cache is None:
        a_out = _sb_prompt(proj3, n_sb_heads)
    else:
        a_out = _sb_sample(proj3, sb_cache[0], sb_cache[1], n_sb_heads)
    seq_tile = _row_tile(seq, MAX_ROW_TILE)
    b_out = _pool(proj3, 3, pool_width, pool_hist, n_hist, lw["pool_w"], lw["pool_scale"], seq_tile)
    c_out, s_new_t = _hgrn(proj3, 2, hg_width, lw["log_lb"], lw["log_1m_lb"], lw["hgrn_norm_g"],
                           jnp.swapaxes(s0, -1, -2), _row_tile(seq, MAX_HGRN_ROWS))
    x = _outproj(a_out.reshape(rows, sb_width), b_out.reshape(rows, pool_width),
                 c_out.reshape(rows, hg_width), lw["w_out"], x, tm)

    h2 = _rmsnorm(x, lw["norm2_g"], BF16)
    act = _gateup(h2, lw["w_gate"], lw["w_up"], lw["conv_w"], lw["conv_b"], tm, seq, conv_hist)
    x = _down(act, lw["w_down"], x, tm)

    h2_last = h2.reshape(batch, seq, d)[:, seq - (CONV_W - 1):].reshape(batch * (CONV_W - 1), d)
    n_last = h2_last.shape[0]
    n_last_pad = -(-n_last // BF16_SUBLANES) * BF16_SUBLANES
    h2_last = jnp.pad(h2_last, ((0, n_last_pad - n_last), (0, 0)))
    conv_state = _matmul(h2_last, lw["w_gate"], n_last_pad)[:n_last].reshape(batch, CONV_W - 1, -1)

    k_new = proj3[:, :, sb_width:2 * sb_width].reshape(batch, seq, n_sb_heads, HEAD_DIM)
    v_new = proj3[:, :, 2 * sb_width:3 * sb_width].reshape(batch, seq, n_sb_heads, HEAD_DIM)
    xb = proj3[:, :, 3 * sb_width:3 * sb_width + pool_width]
    pool_new = jnp.concatenate([pool_hist, xb], axis=1)[:, -POOL_HIST:]
    s_new = jnp.swapaxes(s_new_t, -1, -2)
    return x, (k_new, v_new, s_new, pool_new, conv_state)


def kernel(x_prompt, x_sample, cache_sb_k, cache_sb_v, state_hgrn, state_pool, state_conv, meta_tokens,
           norm1_g, w_in, pool_w, pool_scale, hgrn_lower_bounds, hgrn_norm_g, w_out, norm2_g, ffn_w_gate,
           ffn_w_up, ffn_conv_w, ffn_conv_b, ffn_w_down, final_norm_g):
    bp, seq_p, d = x_prompt.shape
    bs, seq_s, _ = x_sample.shape
    depth = w_in.shape[0]
    tp = N_META + seq_p
    hg_heads = (d // 2) // HEAD_DIM
    d_ff = ffn_w_gate.shape[-1]

    meta = jnp.broadcast_to(meta_tokens[None], (bp, N_META, d))
    xp = jnp.concatenate([meta, x_prompt], axis=1).reshape(bp * tp, d)
    xs = x_sample.reshape(bs * seq_s, d)

    probs = jax.nn.softmax(hgrn_lower_bounds.astype(F32), axis=0)
    lower = jnp.maximum(jnp.cumsum(probs, axis=0) - probs[0], 0.0)
    log_lb = jnp.log(lower)
    log_1m_lb = jnp.log1p(-lower)

    outs_p, outs_s = [], []
    for l in range(depth):
        lw = dict(norm1_g=norm1_g[l], w_in=w_in[l].astype(BF16), pool_w=pool_w[l].astype(BF16),
                  pool_scale=pool_scale[l], log_lb=log_lb[l], log_1m_lb=log_1m_lb[l],
                  hgrn_norm_g=hgrn_norm_g[l], w_out=w_out[l].astype(BF16), norm2_g=norm2_g[l],
                  w_gate=ffn_w_gate[l].astype(BF16), w_up=ffn_w_up[l].astype(BF16),
                  conv_w=ffn_conv_w[l], conv_b=ffn_conv_b[l], w_down=ffn_w_down[l].astype(BF16))
        xp, out = _layer(xp, bp, tp, lw, None, jnp.zeros((bp, POOL_HIST, d // 4), F32), 0,
                         jnp.zeros((bp, hg_heads, HEAD_DIM, HEAD_DIM), F32), None)
        outs_p.append(out)
        cache = (cache_sb_k[l].reshape(bs, -1, d // 4), cache_sb_v[l].reshape(bs, -1, d // 4))
        xs, out = _layer(xs, bs, seq_s, lw, cache, state_pool[l], POOL_HIST, state_hgrn[l], state_conv[l])
        outs_s.append(out)

    y_prompt = _rmsnorm(xp, final_norm_g, F32).reshape(bp, tp, d)[:, N_META:]
    y_sample = _rmsnorm(xs, final_norm_g, F32).reshape(bs, seq_s, d)
    stack = lambda outs, idx: jnp.stack([o[idx] for o in outs])
    return (y_prompt, y_sample,
            stack(outs_p, 0), stack(outs_p, 1), stack(outs_p, 2), stack(outs_p, 3), stack(outs_p, 4),
            stack(outs_s, 0), stack(outs_s, 1), stack(outs_s, 2), stack(outs_s, 3), stack(outs_s, 4))
```

```python
import functools
import math

import jax
import jax.numpy as jnp
from jax import lax
from jax.experimental import pallas as pl
from jax.experimental.pallas import tpu as pltpu

F32 = jnp.float32
BF16 = jnp.bfloat16

N_META = 16
EPS = 1e-6
HEAD_DIM = 128
POOL_WINDOWS = (2, 4, 8, 16)
POOL_HIST = max(POOL_WINDOWS) - 1
CONV_W = 3

VMEM_LIMIT_BYTES = 56 * 1024 * 1024
BF16_SUBLANES = 16
MAX_ROW_TILE = 1024
MAX_NORM_ROWS = 512
MAX_HGRN_ROWS = 256
SB_BLOCK = 256
SB_UNROLL = 4
HG_CHUNK = 16
COL_TILE = 512
LANE = 128
SIGN_BIT = -2 ** 31
LOG2_E = 1.4426950408889634


def _row_tile(rows, limit):
    best = None
    for t in range(BF16_SUBLANES, min(rows, limit) + 1, BF16_SUBLANES):
        if rows % t == 0:
            best = t
    assert best is not None, rows
    return best


def _params(*semantics):
    return pltpu.CompilerParams(dimension_semantics=semantics, vmem_limit_bytes=VMEM_LIMIT_BYTES)


def _sigmoid(x):
    return 1.0 / (1.0 + jnp.exp(-x))


def _rmsnorm_kernel(x_ref, g_ref, o_ref):
    x = x_ref[...]
    ms = jnp.mean(x * x, axis=-1, keepdims=True)
    o_ref[...] = (x * lax.rsqrt(ms + EPS) * g_ref[...]).astype(o_ref.dtype)


def _rmsnorm(x, g, out_dtype):
    rows, d = x.shape
    tr = _row_tile(rows, MAX_NORM_ROWS)
    return pl.pallas_call(
        _rmsnorm_kernel,
        out_shape=jax.ShapeDtypeStruct((rows, d), out_dtype),
        grid=(rows // tr,),
        in_specs=[pl.BlockSpec((tr, d), lambda i: (i, 0)),
                  pl.BlockSpec((1, d), lambda i: (0, 0))],
        out_specs=pl.BlockSpec((tr, d), lambda i: (i, 0)),
        compiler_params=_params("parallel"),
        name="rmsnorm",
    )(x, g.reshape(1, d).astype(F32))


def _matmul_kernel(x_ref, w_ref, o_ref):
    o_ref[...] = jnp.dot(x_ref[...], w_ref[...], preferred_element_type=F32)


def _matmul(x, w, layer, tm):
    rows, k = x.shape
    n = w.shape[2]
    tn = min(2 * COL_TILE, n)
    return pl.pallas_call(
        _matmul_kernel,
        out_shape=jax.ShapeDtypeStruct((rows, n), F32),
        grid=(pl.cdiv(n, tn), rows // tm),
        in_specs=[pl.BlockSpec((tm, k), lambda j, i: (i, 0)),
                  pl.BlockSpec((None, k, tn), lambda j, i: (layer, 0, j))],
        out_specs=pl.BlockSpec((tm, tn), lambda j, i: (i, j)),
        compiler_params=_params("parallel", "parallel"),
        name="matmul",
    )(x, w)


def _outproj_kernel(a_ref, b_ref, c_ref, wa_ref, wb_ref, wc_ref, x_ref, o_ref):
    acc = jnp.dot(a_ref[...], wa_ref[...], preferred_element_type=F32)
    acc += jnp.dot(b_ref[...], wb_ref[...], preferred_element_type=F32)
    acc += jnp.dot(c_ref[...], wc_ref[...], preferred_element_type=F32)
    o_ref[...] = x_ref[...] + acc


def _outproj(a, b, c, w_out, layer, x, tm):
    rows, d = x.shape
    wa, wb, wc = a.shape[1], b.shape[1], c.shape[1]
    assert wa == wb and wc == 2 * wa and w_out.shape[1] == wa + wb + wc
    tn = min(COL_TILE, d)
    return pl.pallas_call(
        _outproj_kernel,
        out_shape=jax.ShapeDtypeStruct((rows, d), F32),
        grid=(d // tn, rows // tm),
        in_specs=[pl.BlockSpec((tm, wa), lambda j, i: (i, 0)),
                  pl.BlockSpec((tm, wb), lambda j, i: (i, 0)),
                  pl.BlockSpec((tm, wc), lambda j, i: (i, 0)),
                  pl.BlockSpec((None, wa, tn), lambda j, i: (layer, 0, j)),
                  pl.BlockSpec((None, wb, tn), lambda j, i: (layer, 1, j)),
                  pl.BlockSpec((None, wc, tn), lambda j, i: (layer, 1, j)),
                  pl.BlockSpec((tm, tn), lambda j, i: (i, j))],
        out_specs=pl.BlockSpec((tm, tn), lambda j, i: (i, j)),
        compiler_params=_params("parallel", "parallel"),
        name="outproj",
    )(a, b, c, w_out, w_out, w_out, x)


def _ffn_act(g, g1, g2, u, cw_ref, cb_ref):
    conv = cb_ref[...] + g2 * cw_ref[0:1, :] + g1 * cw_ref[1:2, :] + g * cw_ref[2:3, :]
    return (conv * _sigmoid(conv) * u).astype(BF16)


def _gateup_carry_kernel(h_ref, wg_ref, wu_ref, cw_ref, cb_ref, o_ref, carry_ref, *, tiles_per_seq):
    i = pl.program_id(1)
    h = h_ref[...]
    g = jnp.dot(h, wg_ref[...], preferred_element_type=F32)
    u = jnp.dot(h, wu_ref[...], preferred_element_type=F32)
    tm = g.shape[0]

    @pl.when(i % tiles_per_seq == 0)
    def _():
        carry_ref[...] = jnp.zeros_like(carry_ref)

    prev = carry_ref[...]
    row = lax.broadcasted_iota(jnp.int32, g.shape, 0)
    g1 = jnp.where(row == 0, prev[7:8, :], pltpu.roll(g, 1, 0))
    g2 = jnp.where(row == 0, prev[6:7, :], jnp.where(row == 1, prev[7:8, :], pltpu.roll(g, 2, 0)))
    carry_ref[...] = g[tm - 8:tm, :]
    o_ref[...] = _ffn_act(g, g1, g2, u, cw_ref, cb_ref)


def _gateup_hist_kernel(h_ref, wg_ref, wu_ref, cw_ref, cb_ref, h1_ref, h2_ref, o_ref, *, seq_len):
    h = h_ref[...]
    g = jnp.dot(h, wg_ref[...], preferred_element_type=F32)
    u = jnp.dot(h, wu_ref[...], preferred_element_type=F32)
    pos = lax.broadcasted_iota(jnp.int32, g.shape, 0) % seq_len
    g1 = jnp.where(pos == 0, h1_ref[...], pltpu.roll(g, 1, 0))
    g2 = jnp.where(pos < 2, h2_ref[...], pltpu.roll(g, 2, 0))
    o_ref[...] = _ffn_act(g, g1, g2, u, cw_ref, cb_ref)


def _gateup(h, w_gate, w_up, layer, conv_w, conv_b, tm, seq_len, hist):
    rows, d = h.shape
    n = w_gate.shape[2]
    tn = min(COL_TILE, n)
    grid = (pl.cdiv(n, tn), rows // tm)
    row_spec = pl.BlockSpec((tm, d), lambda j, i: (i, 0))
    w_spec = pl.BlockSpec((None, d, tn), lambda j, i: (layer, 0, j))
    cw_spec = pl.BlockSpec((CONV_W, tn), lambda j, i: (0, j))
    cb_spec = pl.BlockSpec((1, tn), lambda j, i: (0, j))
    out_spec = pl.BlockSpec((tm, tn), lambda j, i: (i, j))
    out_shape = jax.ShapeDtypeStruct((rows, n), BF16)
    cb = conv_b.reshape(1, n)
    if hist is None:
        assert seq_len % tm == 0
        return pl.pallas_call(
            functools.partial(_gateup_carry_kernel, tiles_per_seq=seq_len // tm),
            out_shape=out_shape, grid=grid,
            in_specs=[row_spec, w_spec, w_spec, cw_spec, cb_spec],
            out_specs=out_spec,
            scratch_shapes=[pltpu.VMEM((8, tn), F32)],
            compiler_params=_params("arbitrary", "arbitrary"),
            name="gateup_carry",
        )(h, w_gate, w_up, conv_w, cb)
    assert tm % seq_len == 0
    nb = rows // seq_len
    zeros = jnp.zeros((nb, seq_len, n), F32)
    h1 = zeros.at[:, 0].set(hist[:, 1]).reshape(rows, n)
    h2 = zeros.at[:, 0].set(hist[:, 0]).at[:, 1].set(hist[:, 1]).reshape(rows, n)
    hist_spec = pl.BlockSpec((tm, tn), lambda j, i: (i, j))
    return pl.pallas_call(
        functools.partial(_gateup_hist_kernel, seq_len=seq_len),
        out_shape=out_shape, grid=grid,
        in_specs=[row_spec, w_spec, w_spec, cw_spec, cb_spec, hist_spec, hist_spec],
        out_specs=out_spec,
        compiler_params=_params("parallel", "parallel"),
        name="gateup_hist",
    )(h, w_gate, w_up, conv_w, cb, h1, h2)


def _down_kernel(a_ref, w_ref, x_ref, o_ref):
    k = pl.program_id(2)
    part = jnp.dot(a_ref[...], w_ref[...], preferred_element_type=F32)

    @pl.when(k == 0)
    def _():
        o_ref[...] = x_ref[...] + part

    @pl.when(k != 0)
    def _():
        o_ref[...] += part


def _down(act, w_down, layer, x, tm):
    rows, n = act.shape
    d = x.shape[1]
    tn = min(COL_TILE, d)
    tk = n // 2 if (n % (2 * LANE) == 0) else n
    return pl.pallas_call(
        _down_kernel,
        out_shape=jax.ShapeDtypeStruct((rows, d), F32),
        grid=(rows // tm, d // tn, n // tk),
        in_specs=[pl.BlockSpec((tm, tk), lambda i, j, k: (i, k)),
                  pl.BlockSpec((None, tk, tn), lambda i, j, k: (layer, k, j)),
                  pl.BlockSpec((tm, tn), lambda i, j, k: (i, j))],
        out_specs=pl.BlockSpec((tm, tn), lambda i, j, k: (i, j)),
        compiler_params=_params("parallel", "parallel", "arbitrary"),
        name="down",
    )(act, w_down, x)


def _sb_logs(z, mask):
    neg_abs = lax.bitcast_convert_type(lax.bitcast_convert_type(z, jnp.int32) | SIGN_BIT, F32)
    sp = jnp.maximum(z, 0.0) + jnp.log2(1.0 + jnp.exp2(neg_abs))
    return z - sp, (sp if mask is None else jnp.where(mask, sp, 0.0))


def _sb_split(drop):
    hi = drop.astype(BF16)
    return hi, (drop - hi.astype(F32)).astype(BF16)


def _sb_prompt_kernel(aq_ref, aj_ref, bq_ref, dq_ref, dj_ref, q_ref, k_ref, v_ref, u_ref, o_ref,
                      qb, kb, vb, acc, car, z_s, beta_s, hi_s, lo_s, w_s,
                      *, scale, t_len, n_steps):
    blk = SB_BLOCK
    t_pad = kb.shape[0]
    nq = t_pad // blk
    nt = (((1,), (1,)), ((), ()))

    for src, dst, mul in ((q_ref, qb, scale), (k_ref, kb, None), (v_ref, vb, None)):
        x = src[0] if mul is None else src[0] * mul
        dst[0:t_len, :] = x.astype(BF16)
        if t_pad > t_len:
            dst[t_len:t_pad, :] = jnp.zeros((t_pad - t_len, HEAD_DIM), BF16)
    for ref in (z_s, beta_s, hi_s, lo_s, w_s):
        ref[...] = jnp.zeros_like(ref)

    row = lax.broadcasted_iota(jnp.int32, (blk, blk), 0)
    col = lax.broadcasted_iota(jnp.int32, (blk, blk), 1)
    causal = col < row

    def rows(i):
        return pl.ds(pl.multiple_of(i * blk, blk), blk)

    def cumsum(hi, lo):
        return (jnp.dot(hi, u_ref[...], preferred_element_type=F32)
                + jnp.dot(lo, u_ref[...], preferred_element_type=F32))

    def diag(qi, c):
        r = rows(qi)
        z = lax.dot_general(qb[r, :], kb[r, :], nt, preferred_element_type=F32)
        beta, drop = _sb_logs(z, causal)
        w = jnp.where(causal, jnp.exp2(beta - cumsum(*_sb_split(drop))), 0.0)
        acc[qi] = jnp.dot(w.astype(BF16), vb[r, :], preferred_element_type=F32)
        car[qi] = jnp.sum(drop, axis=-1, keepdims=True)
        return c

    lax.fori_loop(0, nq, diag, 0)

    def step(t, slot):
        other = 1 - slot
        acc[dq_ref[t]] += jnp.dot(w_s[other], vb[rows(dj_ref[t]), :], preferred_element_type=F32)
        w_s[slot] = jnp.exp2(beta_s[slot] - cumsum(hi_s[slot], lo_s[slot])).astype(BF16)
        z_s[slot] = lax.dot_general(qb[rows(aq_ref[t]), :], kb[rows(aj_ref[t]), :], nt,
                                    preferred_element_type=F32)
        q1 = bq_ref[t]
        beta, drop = _sb_logs(z_s[other], None)
        carry = car[q1]
        beta_s[other] = beta - carry
        hi_s[other], lo_s[other] = _sb_split(drop)
        car[q1] = carry + jnp.sum(drop, axis=-1, keepdims=True)

    def steps(it, c):
        for s in range(SB_UNROLL):
            step(it * SB_UNROLL + s, s % 2)
        return c

    acc[nq] = jnp.zeros((blk, HEAD_DIM), F32)
    car[nq] = jnp.zeros((blk, 1), F32)
    lax.fori_loop(0, n_steps // SB_UNROLL, steps, 0)
    for qi in range(nq):
        r = slice(qi * blk, min((qi + 1) * blk, t_len))
        o_ref[0, r, :] = acc[qi, 0:r.stop - r.start, :].astype(o_ref.dtype)


def _tri_ones(n):
    r = lax.broadcasted_iota(jnp.int32, (n, n), 0)
    c = lax.broadcasted_iota(jnp.int32, (n, n), 1)
    return (r > c).astype(BF16)


def _sb_prompt(proj, n_heads):
    b, t, _ = proj.shape
    blk = SB_BLOCK
    nq = pl.cdiv(t, blk)
    t_pad = nq * blk
    scale = LOG2_E / math.sqrt(HEAD_DIM)
    items = [(qi, j) for qi in range(1, nq) for j in range(qi - 1, -1, -1)]
    n_steps = 0 if not items else -(-(len(items) + 3) // SB_UNROLL) * SB_UNROLL
    item = lambda m: items[m] if 0 <= m < len(items) else None
    table = lambda delay, pick, spare: jnp.asarray(
        [spare if item(s - delay) is None else item(s - delay)[pick] for s in range(n_steps)] + [spare], jnp.int32)
    tables = (table(0, 0, 0), table(0, 1, 0), table(1, 0, nq), table(3, 0, nq), table(3, 1, 0))
    head = lambda off: pl.BlockSpec((1, t, HEAD_DIM), lambda bi, h, *_: (bi, 0, off + h))
    ring = lambda dt: pltpu.VMEM((2, blk, blk), dt)
    return pl.pallas_call(
        functools.partial(_sb_prompt_kernel, scale=scale, t_len=t, n_steps=n_steps),
        out_shape=jax.ShapeDtypeStruct((b, t, n_heads * HEAD_DIM), BF16),
        grid_spec=pltpu.PrefetchScalarGridSpec(
            num_scalar_prefetch=5,
            grid=(b, n_heads),
            in_specs=[head(0), head(n_heads), head(2 * n_heads),
                      pl.BlockSpec((blk, blk), lambda bi, h, *_: (0, 0))],
            out_specs=pl.BlockSpec((1, t, HEAD_DIM), lambda bi, h, *_: (bi, 0, h)),
            scratch_shapes=[pltpu.VMEM((t_pad, HEAD_DIM), BF16)] * 3
                           + [pltpu.VMEM((nq + 1, blk, HEAD_DIM), F32), pltpu.VMEM((nq + 1, blk, 1), F32),
                              ring(F32), ring(F32), ring(BF16), ring(BF16), ring(BF16)]),
        compiler_params=_params("parallel", "parallel"),
        name="sb_prompt",
    )(*tables, proj, proj, proj, _tri_ones(blk))


def _sb_sample_kernel(q_ref, kn_ref, vn_ref, kc_ref, vc_ref, u_ref, o_ref, *, scale, n_cache_blocks):
    blk = SB_BLOCK
    n = q_ref.shape[1]
    nt = (((1,), (1,)), ((), ()))
    q = (q_ref[0] * scale).astype(BF16)
    row = lax.broadcasted_iota(jnp.int32, (n, n), 0)
    col = lax.broadcasted_iota(jnp.int32, (n, n), 1)
    keys = [kn_ref[0].astype(BF16)] + [kc_ref[0, j * blk:(j + 1) * blk, :].astype(BF16)
                                       for j in reversed(range(n_cache_blocks))]
    vals = [vn_ref[0].astype(BF16)] + [vc_ref[0, j * blk:(j + 1) * blk, :].astype(BF16)
                                       for j in reversed(range(n_cache_blocks))]
    masks = [col < row] + [None] * n_cache_blocks
    tris = [u_ref[0:n, 0:n]] + [u_ref[...]] * n_cache_blocks
    logs = [_sb_logs(lax.dot_general(q, k, nt, preferred_element_type=F32), m) for k, m in zip(keys, masks)]
    gaps = []
    for (_, drop), u in zip(logs, tris):
        hi, lo = _sb_split(drop)
        gaps.append(jnp.dot(hi, u, preferred_element_type=F32) + jnp.dot(lo, u, preferred_element_type=F32))
    carry = jnp.zeros((n, 1), F32)
    acc = jnp.zeros((n, HEAD_DIM), F32)
    for (beta, drop), gap, v, m in zip(logs, gaps, vals, masks):
        w = jnp.exp2(beta - gap - carry)
        if m is not None:
            w = jnp.where(m, w, 0.0)
        acc = acc + jnp.dot(w.astype(BF16), v, preferred_element_type=F32)
        carry = carry + jnp.sum(drop, axis=-1, keepdims=True)
    o_ref[0] = acc.astype(o_ref.dtype)


def _sb_sample(proj, cache_k, cache_v, layer, n_heads):
    b, n, _ = proj.shape
    p = cache_k.shape[2]
    blk = SB_BLOCK
    assert p % blk == 0 and n <= blk
    scale = LOG2_E / math.sqrt(HEAD_DIM)
    new_spec = lambda off: pl.BlockSpec((1, n, HEAD_DIM), lambda bi, h: (bi, 0, off + h))
    cache_spec = pl.BlockSpec((None, 1, p, HEAD_DIM), lambda bi, h: (layer, bi, 0, h))
    return pl.pallas_call(
        functools.partial(_sb_sample_kernel, scale=scale, n_cache_blocks=p // blk),
        out_shape=jax.ShapeDtypeStruct((b, n, n_heads * HEAD_DIM), BF16),
        grid=(b, n_heads),
        in_specs=[new_spec(0), new_spec(n_heads), new_spec(2 * n_heads), cache_spec, cache_spec,
                  pl.BlockSpec((blk, blk), lambda bi, h: (0, 0))],
        out_specs=pl.BlockSpec((1, n, HEAD_DIM), lambda bi, h: (bi, 0, h)),
        compiler_params=_params("parallel", "parallel"),
        name="sb_sample",
    )(proj, proj, proj, cache_k, cache_v, _tri_ones(blk))


def _pool_kernel(x_ref, hist_ref, w_ref, scale_ref, o_ref, ext_ref, *, n_hist, group_dim):
    ti = pl.program_id(1)
    tm = x_ref.shape[1]
    pad = POOL_HIST + 1

    @pl.when(ti == 0)
    def _():
        ext_ref[0:pad, :] = hist_ref[0]

    @pl.when(ti != 0)
    def _():
        ext_ref[0:pad, :] = ext_ref[tm:tm + pad, :]

    ext_ref[pad:pad + tm, :] = x_ref[0]
    t = ti * tm + lax.broadcasted_iota(jnp.int32, (tm, 1), 0)
    for gi, win in enumerate(POOL_WINDOWS):
        cols = slice(gi * group_dim, (gi + 1) * group_dim)
        x = ext_ref[pad:pad + tm, cols]
        total = x
        for d in range(1, win):
            total = total + ext_ref[pad - d:pad - d + tm, cols]
        count = jnp.minimum(t + 1 + n_hist, win).astype(F32)
        pooled = total / count - x
        y = jnp.dot(pooled.astype(BF16), w_ref[gi], preferred_element_type=F32)
        o_ref[0, :, cols] = (y * scale_ref[:, cols]).astype(o_ref.dtype)


def _pool(proj, col_block, width, hist, n_hist, w_pool, scale, tm):
    b, t, _ = proj.shape
    groups = len(POOL_WINDOWS)
    group_dim = width // groups
    pad = POOL_HIST + 1
    hist_pad = jnp.concatenate([jnp.zeros((b, 1, width), F32), hist], axis=1)
    return pl.pallas_call(
        functools.partial(_pool_kernel, n_hist=n_hist, group_dim=group_dim),
        out_shape=jax.ShapeDtypeStruct((b, t, width), BF16),
        grid=(b, t // tm),
        in_specs=[pl.BlockSpec((1, tm, width), lambda bi, ti: (bi, ti, col_block)),
                  pl.BlockSpec((1, pad, width), lambda bi, ti: (bi, 0, 0)),
                  pl.BlockSpec((groups, group_dim, group_dim), lambda bi, ti: (0, 0, 0)),
                  pl.BlockSpec((1, width), lambda bi, ti: (0, 0))],
        out_specs=pl.BlockSpec((1, tm, width), lambda bi, ti: (bi, ti, 0)),
        scratch_shapes=[pltpu.VMEM((tm + pad, width), F32)],
        compiler_params=_params("parallel", "arbitrary"),
        name="pool",
    )(proj, hist_pad, w_pool, scale.reshape(1, width))


def _cumsum_rows(x):
    row = lax.broadcasted_iota(jnp.int32, x.shape, 0)
    shift = 1
    while shift < x.shape[0]:
        x = x + jnp.where(row >= shift, pltpu.roll(x, shift, 0), 0.0)
        shift *= 2
    return x


def _hgrn_kernel(q_ref, f_ref, i_ref, g_ref, la_ref, lc_ref, ng_ref, s0_ref, o_ref, sout_ref, s_ref,
                 *, n_heads, n_chunks):
    tb = pl.program_id(1)
    half = HG_CHUNK // 2
    hd = HEAD_DIM

    @pl.when(tb == 0)
    def _():
        s_ref[...] = s0_ref[0]

    la = la_ref[...]
    lc = lc_ref[...]
    ng = ng_ref[...]
    row8 = lax.broadcasted_iota(jnp.int32, (half, hd), 0)

    def chunk(c, carry):
        r0 = pl.multiple_of(c * HG_CHUNK, HG_CHUNK)
        qc = q_ref[0, pl.ds(r0, HG_CHUNK), :]
        fc = f_ref[0, pl.ds(r0, HG_CHUNK), :]
        v = i_ref[0, pl.ds(r0, HG_CHUNK), :]
        gc = g_ref[0, pl.ds(r0, HG_CHUNK), :]
        q = qc * _sigmoid(qc)
        gate = gc * _sigmoid(gc)
        l1 = jnp.log(1.0 + jnp.exp(-jnp.abs(fc)))
        log_sig = jnp.minimum(fc, 0.0) - l1
        log_sig_neg = jnp.minimum(-fc, 0.0) - l1
        b = lc + log_sig
        log_f = jnp.maximum(la, b) + jnp.log(1.0 + jnp.exp(-jnp.abs(la - b)))
        k = jnp.exp(lc + log_sig_neg)
        cum = _cumsum_rows(log_f)
        last = cum[HG_CHUNK - 1:HG_CHUNK, :]
        qt = (q * jnp.exp(cum)).astype(BF16)
        kt = (k * jnp.exp(last - cum)).astype(BF16)
        dec = jnp.exp(last)
        vb = v.astype(BF16)
        for h in range(n_heads):
            sl = slice(h * hd, (h + 1) * hd)
            cum_h, q_h, k_h, v_h = cum[:, sl], q[:, sl], k[:, sl], v[:, sl]
            cum_a, cum_b = cum_h[0:half], cum_h[half:]
            q_a, q_b = q_h[0:half], q_h[half:]
            o_a = jnp.zeros((half, hd), F32)
            o_b = jnp.zeros((half, hd), F32)
            for s in range(HG_CHUNK):
                cs, ks, vs = cum_h[s:s + 1], k_h[s:s + 1], v_h[s:s + 1]
                if s < half:
                    e_a = jnp.where(row8 >= s, jnp.exp(cum_a - cs), 0.0)
                    o_a = o_a + jnp.sum(q_a * e_a * ks, axis=-1, keepdims=True) * vs
                    e_b = jnp.exp(cum_b - cs)
                else:
                    e_b = jnp.where(row8 + half >= s, jnp.exp(cum_b - cs), 0.0)
                o_b = o_b + jnp.sum(q_b * e_b * ks, axis=-1, keepdims=True) * vs
            st = s_ref[h]
            inter = lax.dot_general(qt[:, sl], st.astype(BF16), (((1,), (1,)), ((), ())),
                                    preferred_element_type=F32)
            o = jnp.concatenate([o_a, o_b], axis=0) + inter
            ms = jnp.mean(o * o, axis=-1, keepdims=True)
            o = o * lax.rsqrt(ms + EPS) * ng * gate[:, sl]
            o_ref[0, pl.ds(r0, HG_CHUNK), sl] = o.astype(o_ref.dtype)
            s_ref[h] = st * dec[:, sl] + lax.dot_general(vb[:, sl], kt[:, sl], (((0,), (0,)), ((), ())),
                                                         preferred_element_type=F32)
        return carry

    lax.fori_loop(0, n_chunks, chunk, 0)

    @pl.when(tb == pl.num_programs(1) - 1)
    def _():
        sout_ref[0] = s_ref[...]


def _hgrn(proj, first_block, width, log_lb, log_1m_lb, norm_g, s0_t, tb_rows):
    b, t, _ = proj.shape
    n_heads = width // HEAD_DIM
    assert t % tb_rows == 0 and tb_rows % HG_CHUNK == 0
    col = lambda off: pl.BlockSpec((1, tb_rows, width), lambda bi, ti: (bi, ti, first_block + off))
    vec = pl.BlockSpec((1, width), lambda bi, ti: (0, 0))
    state_spec = pl.BlockSpec((1, n_heads, HEAD_DIM, HEAD_DIM), lambda bi, ti: (bi, 0, 0, 0))
    return pl.pallas_call(
        functools.partial(_hgrn_kernel, n_heads=n_heads, n_chunks=tb_rows // HG_CHUNK),
        out_shape=(jax.ShapeDtypeStruct((b, t, width), BF16),
                   jax.ShapeDtypeStruct((b, n_heads, HEAD_DIM, HEAD_DIM), F32)),
        grid=(b, t // tb_rows),
        in_specs=[col(0), col(1), col(2), col(3), vec, vec,
                  pl.BlockSpec((1, HEAD_DIM), lambda bi, ti: (0, 0)), state_spec],
        out_specs=(pl.BlockSpec((1, tb_rows, width), lambda bi, ti: (bi, ti, 0)), state_spec),
        scratch_shapes=[pltpu.VMEM((n_heads, HEAD_DIM, HEAD_DIM), F32)],
        compiler_params=_params("parallel", "arbitrary"),
        name="hgrn",
    )(proj, proj, proj, proj, log_lb.reshape(1, width), log_1m_lb.reshape(1, width),
      norm_g.reshape(1, HEAD_DIM).astype(F32), s0_t)


def _layer(x, batch, seq, lw, layer, sb_cache, pool_hist, n_hist, s0, conv_hist):
    rows, d = x.shape
    sb_width = d // 4
    pool_width = d // 4
    hg_width = d // 2
    n_sb_heads = sb_width // HEAD_DIM
    tm = _row_tile(seq, MAX_ROW_TILE) if conv_hist is None else rows
    assert rows % tm == 0

    h = _rmsnorm(x, lw["norm1_g"], BF16)
    proj = _matmul(h, lw["w_in"], layer, tm)
    proj3 = proj.reshape(batch, seq, 3 * d)
    if sb_cache is None:
        a_out = _sb_prompt(proj3, n_sb_heads)
    else:
        a_out = _sb_sample(proj3, sb_cache[0], sb_cache[1], layer, n_sb_heads)
    seq_tile = _row_tile(seq, MAX_ROW_TILE)
    b_out = _pool(proj3, 3, pool_width, pool_hist, n_hist, lw["pool_w"], lw["pool_scale"], seq_tile)
    c_out, s_new_t = _hgrn(proj3, 2, hg_width, lw["log_lb"], lw["log_1m_lb"], lw["hgrn_norm_g"],
                           jnp.swapaxes(s0, -1, -2), _row_tile(seq, MAX_HGRN_ROWS))
    x = _outproj(a_out.reshape(rows, sb_width), b_out.reshape(rows, pool_width),
                 c_out.reshape(rows, hg_width), lw["w_out"], layer, x, tm)

    h2 = _rmsnorm(x, lw["norm2_g"], BF16)
    act = _gateup(h2, lw["w_gate"], lw["w_up"], layer, lw["conv_w"], lw["conv_b"], tm, seq, conv_hist)
    x = _down(act, lw["w_down"], layer, x, tm)

    h2_last = h2.reshape(batch, seq, d)[:, seq - (CONV_W - 1):].reshape(batch * (CONV_W - 1), d)
    n_last = h2_last.shape[0]
    n_last_pad = -(-n_last // BF16_SUBLANES) * BF16_SUBLANES
    h2_last = jnp.pad(h2_last, ((0, n_last_pad - n_last), (0, 0)))
    conv_state = _matmul(h2_last, lw["w_gate"], layer, n_last_pad)[:n_last].reshape(batch, CONV_W - 1, -1)

    k_new = proj3[:, :, sb_width:2 * sb_width].reshape(batch, seq, n_sb_heads, HEAD_DIM)
    v_new = proj3[:, :, 2 * sb_width:3 * sb_width].reshape(batch, seq, n_sb_heads, HEAD_DIM)
    xb = proj3[:, :, 3 * sb_width:3 * sb_width + pool_width]
    pool_new = jnp.concatenate([pool_hist, xb], axis=1)[:, -POOL_HIST:]
    s_new = jnp.swapaxes(s_new_t, -1, -2)
    return x, (k_new, v_new, s_new, pool_new, conv_state)


def kernel(x_prompt, x_sample, cache_sb_k, cache_sb_v, state_hgrn, state_pool, state_conv, meta_tokens,
           norm1_g, w_in, pool_w, pool_scale, hgrn_lower_bounds, hgrn_norm_g, w_out, norm2_g, ffn_w_gate,
           ffn_w_up, ffn_conv_w, ffn_conv_b, ffn_w_down, final_norm_g):
    bp, seq_p, d = x_prompt.shape
    bs, seq_s, _ = x_sample.shape
    depth = w_in.shape[0]
    tp = N_META + seq_p
    hg_heads = (d // 2) // HEAD_DIM

    meta = jnp.broadcast_to(meta_tokens[None], (bp, N_META, d))
    xp = jnp.concatenate([meta, x_prompt], axis=1).reshape(bp * tp, d)
    xs = x_sample.reshape(bs * seq_s, d)

    probs = jax.nn.softmax(hgrn_lower_bounds.astype(F32), axis=0)
    lower = jnp.maximum(jnp.cumsum(probs, axis=0) - probs[0], 0.0)
    log_lb = jnp.log(lower)
    log_1m_lb = jnp.log1p(-lower)

    w_in_b, w_out_b = w_in.astype(BF16), w_out.astype(BF16)
    w_gate_b, w_up_b, w_down_b = ffn_w_gate.astype(BF16), ffn_w_up.astype(BF16), ffn_w_down.astype(BF16)
    pool_w_b = pool_w.astype(BF16)
    cache = (cache_sb_k.reshape(depth, bs, -1, d // 4), cache_sb_v.reshape(depth, bs, -1, d // 4))

    outs_p, outs_s = [], []
    for l in range(depth):
        lw = dict(norm1_g=norm1_g[l], w_in=w_in_b, pool_w=pool_w_b[l], pool_scale=pool_scale[l],
                  log_lb=log_lb[l], log_1m_lb=log_1m_lb[l], hgrn_norm_g=hgrn_norm_g[l], w_out=w_out_b,
                  norm2_g=norm2_g[l], w_gate=w_gate_b, w_up=w_up_b, conv_w=ffn_conv_w[l], conv_b=ffn_conv_b[l],
                  w_down=w_down_b)
        xp, out = _layer(xp, bp, tp, lw, l, None, jnp.zeros((bp, POOL_HIST, d // 4), F32), 0,
                         jnp.zeros((bp, hg_heads, HEAD_DIM, HEAD_DIM), F32), None)
        outs_p.append(out)
        xs, out = _layer(xs, bs, seq_s, lw, l, cache, state_pool[l], POOL_HIST, state_hgrn[l], state_conv[l])
        outs_s.append(out)

    y_prompt = _rmsnorm(xp, final_norm_g, F32).reshape(bp, tp, d)[:, N_META:]
    y_sample = _rmsnorm(xs, final_norm_g, F32).reshape(bs, seq_s, d)
    stack = lambda outs, idx: jnp.stack([o[idx] for o in outs])
    return (y_prompt, y_sample,
            stack(outs_p, 0), stack(outs_p, 1), stack(outs_p, 2), stack(outs_p, 3), stack(outs_p, 4),
            stack(outs_s, 0), stack(outs_s, 1), stack(outs_s, 2), stack(outs_s, 3), stack(outs_s, 4))
```

```python
import functools
import math

import jax
import jax.numpy as jnp
from jax import lax
from jax.experimental import pallas as pl
from jax.experimental.pallas import tpu as pltpu

F32 = jnp.float32
BF16 = jnp.bfloat16

N_META = 16
EPS = 1e-6
HEAD_DIM = 128
POOL_WINDOWS = (2, 4, 8, 16)
POOL_HIST = max(POOL_WINDOWS) - 1
CONV_W = 3

VMEM_LIMIT_BYTES = 56 * 1024 * 1024
BF16_SUBLANES = 16
MAX_ROW_TILE = 1024
MAX_NORM_ROWS = 512
MAX_HGRN_ROWS = 256
SB_BLOCK = 256
SB_UNROLL = 6
HG_CHUNK = 16
COL_TILE = 512
LANE = 128
SIGN_BIT = -2 ** 31
LOG2_E = 1.4426950408889634


def _row_tile(rows, limit):
    best = None
    for t in range(BF16_SUBLANES, min(rows, limit) + 1, BF16_SUBLANES):
        if rows % t == 0:
            best = t
    assert best is not None, rows
    return best


def _params(*semantics):
    return pltpu.CompilerParams(dimension_semantics=semantics, vmem_limit_bytes=VMEM_LIMIT_BYTES)


def _sigmoid(x):
    return 1.0 / (1.0 + jnp.exp(-x))


def _rmsnorm_kernel(x_ref, g_ref, o_ref):
    x = x_ref[...]
    ms = jnp.mean(x * x, axis=-1, keepdims=True)
    o_ref[...] = (x * lax.rsqrt(ms + EPS) * g_ref[...]).astype(o_ref.dtype)


def _rmsnorm(x, g, out_dtype):
    rows, d = x.shape
    tr = _row_tile(rows, MAX_NORM_ROWS)
    return pl.pallas_call(
        _rmsnorm_kernel,
        out_shape=jax.ShapeDtypeStruct((rows, d), out_dtype),
        grid=(rows // tr,),
        in_specs=[pl.BlockSpec((tr, d), lambda i: (i, 0)),
                  pl.BlockSpec((1, d), lambda i: (0, 0))],
        out_specs=pl.BlockSpec((tr, d), lambda i: (i, 0)),
        compiler_params=_params("parallel"),
        name="rmsnorm",
    )(x, g.reshape(1, d).astype(F32))


def _matmul_kernel(x_ref, w_ref, o_ref):
    o_ref[...] = jnp.dot(x_ref[...], w_ref[...], preferred_element_type=F32)


def _matmul(x, w, layer, tm):
    rows, k = x.shape
    n = w.shape[2]
    tn = min(2 * COL_TILE, n)
    return pl.pallas_call(
        _matmul_kernel,
        out_shape=jax.ShapeDtypeStruct((rows, n), F32),
        grid=(pl.cdiv(n, tn), rows // tm),
        in_specs=[pl.BlockSpec((tm, k), lambda j, i: (i, 0)),
                  pl.BlockSpec((None, k, tn), lambda j, i: (layer, 0, j))],
        out_specs=pl.BlockSpec((tm, tn), lambda j, i: (i, j)),
        compiler_params=_params("parallel", "parallel"),
        name="matmul",
    )(x, w)


def _outproj_kernel(a_ref, b_ref, c_ref, wa_ref, wb_ref, wc_ref, x_ref, o_ref):
    acc = jnp.dot(a_ref[...], wa_ref[...], preferred_element_type=F32)
    acc += jnp.dot(b_ref[...], wb_ref[...], preferred_element_type=F32)
    acc += jnp.dot(c_ref[...], wc_ref[...], preferred_element_type=F32)
    o_ref[...] = x_ref[...] + acc


def _outproj(a, b, c, w_out, layer, x, tm):
    rows, d = x.shape
    wa, wb, wc = a.shape[1], b.shape[1], c.shape[1]
    assert wa == wb and wc == 2 * wa and w_out.shape[1] == wa + wb + wc
    tn = min(2 * COL_TILE, d)
    return pl.pallas_call(
        _outproj_kernel,
        out_shape=jax.ShapeDtypeStruct((rows, d), F32),
        grid=(d // tn, rows // tm),
        in_specs=[pl.BlockSpec((tm, wa), lambda j, i: (i, 0)),
                  pl.BlockSpec((tm, wb), lambda j, i: (i, 0)),
                  pl.BlockSpec((tm, wc), lambda j, i: (i, 0)),
                  pl.BlockSpec((None, wa, tn), lambda j, i: (layer, 0, j)),
                  pl.BlockSpec((None, wb, tn), lambda j, i: (layer, 1, j)),
                  pl.BlockSpec((None, wc, tn), lambda j, i: (layer, 1, j)),
                  pl.BlockSpec((tm, tn), lambda j, i: (i, j))],
        out_specs=pl.BlockSpec((tm, tn), lambda j, i: (i, j)),
        compiler_params=_params("parallel", "parallel"),
        name="outproj",
    )(a, b, c, w_out, w_out, w_out, x)


def _ffn_act(g, g1, g2, u, cw_ref, cb_ref):
    conv = cb_ref[...] + g2 * cw_ref[0:1, :] + g1 * cw_ref[1:2, :] + g * cw_ref[2:3, :]
    return (conv * _sigmoid(conv) * u).astype(BF16)


def _gateup_carry_kernel(h_ref, wg_ref, wu_ref, cw_ref, cb_ref, o_ref, g_ref, *, tiles_per_seq):
    i = pl.program_id(1)
    tm = h_ref.shape[0]

    @pl.when(i % tiles_per_seq == 0)
    def _():
        g_ref[0:8, :] = jnp.zeros((8, g_ref.shape[1]), F32)

    @pl.when(i % tiles_per_seq != 0)
    def _():
        g_ref[0:8, :] = g_ref[tm:tm + 8, :]

    h = h_ref[...]
    g_ref[8:tm + 8, :] = jnp.dot(h, wg_ref[...], preferred_element_type=F32)
    u = jnp.dot(h, wu_ref[...], preferred_element_type=F32)
    o_ref[...] = _ffn_act(g_ref[8:tm + 8, :], g_ref[7:tm + 7, :], g_ref[6:tm + 6, :], u, cw_ref, cb_ref)


def _gateup_hist_kernel(h_ref, wg_ref, wu_ref, cw_ref, cb_ref, h1_ref, h2_ref, o_ref, *, seq_len):
    h = h_ref[...]
    g = jnp.dot(h, wg_ref[...], preferred_element_type=F32)
    u = jnp.dot(h, wu_ref[...], preferred_element_type=F32)
    pos = lax.broadcasted_iota(jnp.int32, g.shape, 0) % seq_len
    g1 = jnp.where(pos == 0, h1_ref[...], pltpu.roll(g, 1, 0))
    g2 = jnp.where(pos < 2, h2_ref[...], pltpu.roll(g, 2, 0))
    o_ref[...] = _ffn_act(g, g1, g2, u, cw_ref, cb_ref)


def _gateup(h, w_gate, w_up, layer, conv_w, conv_b, tm, seq_len, hist):
    rows, d = h.shape
    n = w_gate.shape[2]
    tn = min(COL_TILE, n)
    grid = (pl.cdiv(n, tn), rows // tm)
    row_spec = pl.BlockSpec((tm, d), lambda j, i: (i, 0))
    w_spec = pl.BlockSpec((None, d, tn), lambda j, i: (layer, 0, j))
    cw_spec = pl.BlockSpec((CONV_W, tn), lambda j, i: (0, j))
    cb_spec = pl.BlockSpec((1, tn), lambda j, i: (0, j))
    out_spec = pl.BlockSpec((tm, tn), lambda j, i: (i, j))
    out_shape = jax.ShapeDtypeStruct((rows, n), BF16)
    cb = conv_b.reshape(1, n)
    if hist is None:
        assert seq_len % tm == 0
        return pl.pallas_call(
            functools.partial(_gateup_carry_kernel, tiles_per_seq=seq_len // tm),
            out_shape=out_shape, grid=grid,
            in_specs=[row_spec, w_spec, w_spec, cw_spec, cb_spec],
            out_specs=out_spec,
            scratch_shapes=[pltpu.VMEM((tm + 8, tn), F32)],
            compiler_params=_params("arbitrary", "arbitrary"),
            name="gateup_carry",
        )(h, w_gate, w_up, conv_w, cb)
    assert tm % seq_len == 0
    nb = rows // seq_len
    zeros = jnp.zeros((nb, seq_len, n), F32)
    h1 = zeros.at[:, 0].set(hist[:, 1]).reshape(rows, n)
    h2 = zeros.at[:, 0].set(hist[:, 0]).at[:, 1].set(hist[:, 1]).reshape(rows, n)
    hist_spec = pl.BlockSpec((tm, tn), lambda j, i: (i, j))
    return pl.pallas_call(
        functools.partial(_gateup_hist_kernel, seq_len=seq_len),
        out_shape=out_shape, grid=grid,
        in_specs=[row_spec, w_spec, w_spec, cw_spec, cb_spec, hist_spec, hist_spec],
        out_specs=out_spec,
        compiler_params=_params("parallel", "parallel"),
        name="gateup_hist",
    )(h, w_gate, w_up, conv_w, cb, h1, h2)


def _down_kernel(a_ref, w_ref, x_ref, o_ref):
    o_ref[...] = x_ref[...] + jnp.dot(a_ref[...], w_ref[...], preferred_element_type=F32)


def _down(act, w_down, layer, x, tm):
    rows, n = act.shape
    d = x.shape[1]
    tn = min(COL_TILE // 2, d)
    return pl.pallas_call(
        _down_kernel,
        out_shape=jax.ShapeDtypeStruct((rows, d), F32),
        grid=(rows // tm, d // tn),
        in_specs=[pl.BlockSpec((tm, n), lambda i, j: (i, 0), pipeline_mode=pl.Buffered(1)),
                  pl.BlockSpec((None, n, tn), lambda i, j: (layer, 0, j)),
                  pl.BlockSpec((tm, tn), lambda i, j: (i, j))],
        out_specs=pl.BlockSpec((tm, tn), lambda i, j: (i, j)),
        compiler_params=_params("parallel", "arbitrary"),
        name="down",
    )(act, w_down, x)


def _sb_logs(z, mask):
    neg_abs = lax.bitcast_convert_type(lax.bitcast_convert_type(z, jnp.int32) | SIGN_BIT, F32)
    sp = jnp.maximum(z, 0.0) + jnp.log2(1.0 + jnp.exp2(neg_abs))
    return z - sp, (sp if mask is None else jnp.where(mask, sp, 0.0))


def _sb_split(drop):
    hi = drop.astype(BF16)
    return hi, (drop - hi.astype(F32)).astype(BF16)


def _sb_prompt_kernel(aq_ref, aj_ref, bq_ref, dq_ref, dj_ref, q_ref, k_ref, v_ref, u_ref, o_ref,
                      qb, kb, vb, acc, car, z_s, beta_s, hi_s, lo_s, w_s,
                      *, scale, t_len, n_steps):
    blk = SB_BLOCK
    t_pad = kb.shape[0]
    nq = t_pad // blk
    nt = (((1,), (1,)), ((), ()))

    for src, dst, mul in ((q_ref, qb, scale), (k_ref, kb, None), (v_ref, vb, None)):
        x = src[0] if mul is None else src[0] * mul
        dst[0:t_len, :] = x.astype(BF16)
        if t_pad > t_len:
            dst[t_len:t_pad, :] = jnp.zeros((t_pad - t_len, HEAD_DIM), BF16)
    for ref in (z_s, beta_s, hi_s, lo_s, w_s):
        ref[...] = jnp.zeros_like(ref)

    row = lax.broadcasted_iota(jnp.int32, (blk, blk), 0)
    col = lax.broadcasted_iota(jnp.int32, (blk, blk), 1)
    causal = col < row

    def rows(i):
        return pl.ds(pl.multiple_of(i * blk, blk), blk)

    def cumsum(hi, lo):
        return (jnp.dot(hi, u_ref[...], preferred_element_type=F32)
                + jnp.dot(lo, u_ref[...], preferred_element_type=F32))

    def diag(blocks):
        rs = [rows(qi) for qi in blocks]
        logs = [_sb_logs(lax.dot_general(qb[r, :], kb[r, :], nt, preferred_element_type=F32), causal) for r in rs]
        gaps = [cumsum(*_sb_split(drop)) for _, drop in logs]
        for qi, r, (beta, drop), gap in zip(blocks, rs, logs, gaps):
            w = jnp.where(causal, jnp.exp2(beta - gap), 0.0)
            acc[qi] = jnp.dot(w.astype(BF16), vb[r, :], preferred_element_type=F32)
            car[qi] = jnp.sum(drop, axis=-1, keepdims=True)

    def diag_pair(i, c):
        diag([2 * i, 2 * i + 1])
        return c

    lax.fori_loop(0, nq // 2, diag_pair, 0)
    if nq % 2:
        diag([nq - 1])

    def step(t, slot):
        other = 1 - slot
        acc[dq_ref[t]] += jnp.dot(w_s[other], vb[rows(dj_ref[t]), :], preferred_element_type=F32)
        w_s[slot] = jnp.exp2(beta_s[slot] - cumsum(hi_s[slot], lo_s[slot])).astype(BF16)
        z_s[slot] = lax.dot_general(qb[rows(aq_ref[t]), :], kb[rows(aj_ref[t]), :], nt,
                                    preferred_element_type=F32)
        q1 = bq_ref[t]
        beta, drop = _sb_logs(z_s[other], None)
        carry = car[q1]
        beta_s[other] = beta - carry
        hi_s[other], lo_s[other] = _sb_split(drop)
        car[q1] = carry + jnp.sum(drop, axis=-1, keepdims=True)

    def steps(it, c):
        for s in range(SB_UNROLL):
            step(it * SB_UNROLL + s, s % 2)
        return c

    acc[nq] = jnp.zeros((blk, HEAD_DIM), F32)
    car[nq] = jnp.zeros((blk, 1), F32)
    lax.fori_loop(0, n_steps // SB_UNROLL, steps, 0)
    for qi in range(nq):
        r = slice(qi * blk, min((qi + 1) * blk, t_len))
        o_ref[0, r, :] = acc[qi, 0:r.stop - r.start, :].astype(o_ref.dtype)


def _tri_ones(n):
    r = lax.broadcasted_iota(jnp.int32, (n, n), 0)
    c = lax.broadcasted_iota(jnp.int32, (n, n), 1)
    return (r > c).astype(BF16)


def _sb_prompt(proj, n_heads):
    b, t, _ = proj.shape
    blk = SB_BLOCK
    nq = pl.cdiv(t, blk)
    t_pad = nq * blk
    scale = LOG2_E / math.sqrt(HEAD_DIM)
    items = [(qi, j) for qi in range(1, nq) for j in range(qi - 1, -1, -1)]
    n_steps = 0 if not items else -(-(len(items) + 3) // SB_UNROLL) * SB_UNROLL
    item = lambda m: items[m] if 0 <= m < len(items) else None
    table = lambda delay, pick, spare: jnp.asarray(
        [spare if item(s - delay) is None else item(s - delay)[pick] for s in range(n_steps)] + [spare], jnp.int32)
    tables = (table(0, 0, 0), table(0, 1, 0), table(1, 0, nq), table(3, 0, nq), table(3, 1, 0))
    head = lambda off: pl.BlockSpec((1, t, HEAD_DIM), lambda bi, h, *_: (bi, 0, off + h))
    ring = lambda dt: pltpu.VMEM((2, blk, blk), dt)
    return pl.pallas_call(
        functools.partial(_sb_prompt_kernel, scale=scale, t_len=t, n_steps=n_steps),
        out_shape=jax.ShapeDtypeStruct((b, t, n_heads * HEAD_DIM), BF16),
        grid_spec=pltpu.PrefetchScalarGridSpec(
            num_scalar_prefetch=5,
            grid=(b, n_heads),
            in_specs=[head(0), head(n_heads), head(2 * n_heads),
                      pl.BlockSpec((blk, blk), lambda bi, h, *_: (0, 0))],
            out_specs=pl.BlockSpec((1, t, HEAD_DIM), lambda bi, h, *_: (bi, 0, h)),
            scratch_shapes=[pltpu.VMEM((t_pad, HEAD_DIM), BF16)] * 3
                           + [pltpu.VMEM((nq + 1, blk, HEAD_DIM), F32), pltpu.VMEM((nq + 1, blk, 1), F32),
                              ring(F32), ring(F32), ring(BF16), ring(BF16), ring(BF16)]),
        compiler_params=_params("parallel", "parallel"),
        name="sb_prompt",
    )(*tables, proj, proj, proj, _tri_ones(blk))


def _sb_sample_kernel(q_ref, kn_ref, vn_ref, kc_ref, vc_ref, u_ref, o_ref, *, scale, n_cache_blocks):
    blk = SB_BLOCK
    n = q_ref.shape[1]
    nt = (((1,), (1,)), ((), ()))
    q = (q_ref[0] * scale).astype(BF16)
    row = lax.broadcasted_iota(jnp.int32, (n, n), 0)
    col = lax.broadcasted_iota(jnp.int32, (n, n), 1)
    keys = [kn_ref[0].astype(BF16)] + [kc_ref[0, j * blk:(j + 1) * blk, :].astype(BF16)
                                       for j in reversed(range(n_cache_blocks))]
    vals = [vn_ref[0].astype(BF16)] + [vc_ref[0, j * blk:(j + 1) * blk, :].astype(BF16)
                                       for j in reversed(range(n_cache_blocks))]
    masks = [col < row] + [None] * n_cache_blocks
    tris = [u_ref[0:n, 0:n]] + [u_ref[...]] * n_cache_blocks
    logs = [_sb_logs(lax.dot_general(q, k, nt, preferred_element_type=F32), m) for k, m in zip(keys, masks)]
    gaps = []
    for (_, drop), u in zip(logs, tris):
        hi, lo = _sb_split(drop)
        gaps.append(jnp.dot(hi, u, preferred_element_type=F32) + jnp.dot(lo, u, preferred_element_type=F32))
    carry = jnp.zeros((n, 1), F32)
    acc = jnp.zeros((n, HEAD_DIM), F32)
    for (beta, drop), gap, v, m in zip(logs, gaps, vals, masks):
        w = jnp.exp2(beta - gap - carry)
        if m is not None:
            w = jnp.where(m, w, 0.0)
        acc = acc + jnp.dot(w.astype(BF16), v, preferred_element_type=F32)
        carry = carry + jnp.sum(drop, axis=-1, keepdims=True)
    o_ref[0] = acc.astype(o_ref.dtype)


def _sb_sample(proj, cache_k, cache_v, layer, n_heads):
    b, n, _ = proj.shape
    p = cache_k.shape[2]
    blk = SB_BLOCK
    assert p % blk == 0 and n <= blk
    scale = LOG2_E / math.sqrt(HEAD_DIM)
    new_spec = lambda off: pl.BlockSpec((1, n, HEAD_DIM), lambda bi, h: (bi, 0, off + h))
    cache_spec = pl.BlockSpec((None, 1, p, HEAD_DIM), lambda bi, h: (layer, bi, 0, h))
    return pl.pallas_call(
        functools.partial(_sb_sample_kernel, scale=scale, n_cache_blocks=p // blk),
        out_shape=jax.ShapeDtypeStruct((b, n, n_heads * HEAD_DIM), BF16),
        grid=(b, n_heads),
        in_specs=[new_spec(0), new_spec(n_heads), new_spec(2 * n_heads), cache_spec, cache_spec,
                  pl.BlockSpec((blk, blk), lambda bi, h: (0, 0))],
        out_specs=pl.BlockSpec((1, n, HEAD_DIM), lambda bi, h: (bi, 0, h)),
        compiler_params=_params("parallel", "parallel"),
        name="sb_sample",
    )(proj, proj, proj, cache_k, cache_v, _tri_ones(blk))


def _pool_kernel(x_ref, hist_ref, w_ref, scale_ref, o_ref, ext_ref, *, n_hist, group_dim):
    ti = pl.program_id(1)
    tm = x_ref.shape[1]
    pad = POOL_HIST + 1

    @pl.when(ti == 0)
    def _():
        ext_ref[0:pad, :] = hist_ref[0]

    @pl.when(ti != 0)
    def _():
        ext_ref[0:pad, :] = ext_ref[tm:tm + pad, :]

    ext_ref[pad:pad + tm, :] = x_ref[0]
    t = ti * tm + lax.broadcasted_iota(jnp.int32, (tm, 1), 0)
    for gi, win in enumerate(POOL_WINDOWS):
        cols = slice(gi * group_dim, (gi + 1) * group_dim)
        x = ext_ref[pad:pad + tm, cols]
        total = x
        for d in range(1, win):
            total = total + ext_ref[pad - d:pad - d + tm, cols]
        count = jnp.minimum(t + 1 + n_hist, win).astype(F32)
        pooled = total / count - x
        y = jnp.dot(pooled.astype(BF16), w_ref[gi], preferred_element_type=F32)
        o_ref[0, :, cols] = (y * scale_ref[:, cols]).astype(o_ref.dtype)


def _pool(proj, col_block, width, hist, n_hist, w_pool, scale, tm):
    b, t, _ = proj.shape
    groups = len(POOL_WINDOWS)
    group_dim = width // groups
    pad = POOL_HIST + 1
    hist_pad = jnp.concatenate([jnp.zeros((b, 1, width), F32), hist], axis=1)
    return pl.pallas_call(
        functools.partial(_pool_kernel, n_hist=n_hist, group_dim=group_dim),
        out_shape=jax.ShapeDtypeStruct((b, t, width), BF16),
        grid=(b, t // tm),
        in_specs=[pl.BlockSpec((1, tm, width), lambda bi, ti: (bi, ti, col_block)),
                  pl.BlockSpec((1, pad, width), lambda bi, ti: (bi, 0, 0)),
                  pl.BlockSpec((groups, group_dim, group_dim), lambda bi, ti: (0, 0, 0)),
                  pl.BlockSpec((1, width), lambda bi, ti: (0, 0))],
        out_specs=pl.BlockSpec((1, tm, width), lambda bi, ti: (bi, ti, 0)),
        scratch_shapes=[pltpu.VMEM((tm + pad, width), F32)],
        compiler_params=_params("parallel", "arbitrary"),
        name="pool",
    )(proj, hist_pad, w_pool, scale.reshape(1, width))


def _cumsum_rows(x):
    row = lax.broadcasted_iota(jnp.int32, x.shape, 0)
    shift = 1
    while shift < x.shape[0]:
        x = x + jnp.where(row >= shift, pltpu.roll(x, shift, 0), 0.0)
        shift *= 2
    return x


def _hgrn_kernel(q_ref, f_ref, i_ref, g_ref, la_ref, lc_ref, ng_ref, ones_ref, s0_ref, o_ref, sout_ref,
                 s_ref, ck_ref, vs_ref, *, n_heads, n_chunks):
    tb = pl.program_id(1)
    half = HG_CHUNK // 2
    hd = HEAD_DIM

    @pl.when(tb == 0)
    def _():
        s_ref[...] = s0_ref[0]

    la = la_ref[...]
    lc = lc_ref[...]
    ng = ng_ref[...]
    row8 = lax.broadcasted_iota(jnp.int32, (half, hd), 0)

    def chunk(c, carry):
        r0 = pl.multiple_of(c * HG_CHUNK, HG_CHUNK)
        qc = q_ref[0, pl.ds(r0, HG_CHUNK), :]
        fc = f_ref[0, pl.ds(r0, HG_CHUNK), :]
        gc = g_ref[0, pl.ds(r0, HG_CHUNK), :]
        q = qc * _sigmoid(qc)
        gate = gc * _sigmoid(gc)
        l1 = jnp.log(1.0 + jnp.exp(-jnp.abs(fc)))
        log_sig = jnp.minimum(fc, 0.0) - l1
        log_sig_neg = jnp.minimum(-fc, 0.0) - l1
        b = lc + log_sig
        log_f = jnp.maximum(la, b) + jnp.log(1.0 + jnp.exp(-jnp.abs(la - b)))
        log2_k = (lc + log_sig_neg) * LOG2_E
        cum = _cumsum_rows(log_f * LOG2_E)
        last = cum[HG_CHUNK - 1:HG_CHUNK, :]
        qt = (q * jnp.exp2(cum)).astype(BF16)
        kt = jnp.exp2(log2_k + last - cum).astype(BF16)
        dec = jnp.exp2(last)
        ck_ref[...] = cum - log2_k
        v = i_ref[0, pl.ds(r0, HG_CHUNK), :]
        vs_ref[...] = v
        vb = v.astype(BF16)
        for h in range(n_heads):
            sl = slice(h * hd, (h + 1) * hd)
            cum_a, cum_b = cum[0:half, sl], cum[half:, sl]
            q_a, q_b = q[0:half, sl], q[half:, sl]
            pairs = []
            for s in range(HG_CHUNK):
                cs = jnp.broadcast_to(ck_ref[s:s + 1, sl], (half, hd))
                if s < half:
                    pairs.append(q_a * jnp.where(row8 >= s, jnp.exp2(cum_a - cs), 0.0))
                    pairs.append(q_b * jnp.exp2(cum_b - cs))
                else:
                    pairs.append(q_b * jnp.where(row8 + half >= s, jnp.exp2(cum_b - cs), 0.0))
            scores = jnp.dot(jnp.concatenate(pairs, axis=0).astype(BF16), ones_ref[...],
                             preferred_element_type=F32)
            o_a = jnp.zeros((half, hd), F32)
            o_b = jnp.zeros((half, hd), F32)
            blocks = iter(range(len(pairs)))
            for s in range(HG_CHUNK):
                vs = jnp.broadcast_to(vs_ref[s:s + 1, sl], (half, hd))
                if s < half:
                    i = next(blocks)
                    o_a = o_a + scores[i * half:(i + 1) * half] * vs
                i = next(blocks)
                o_b = o_b + scores[i * half:(i + 1) * half] * vs
            st = s_ref[h]
            inter = lax.dot_general(qt[:, sl], st.astype(BF16), (((1,), (1,)), ((), ())),
                                    preferred_element_type=F32)
            o = jnp.concatenate([o_a, o_b], axis=0) + inter
            ms = jnp.mean(o * o, axis=-1, keepdims=True)
            o = o * lax.rsqrt(ms + EPS) * ng * gate[:, sl]
            o_ref[0, pl.ds(r0, HG_CHUNK), sl] = o.astype(o_ref.dtype)
            s_ref[h] = st * dec[:, sl] + lax.dot_general(vb[:, sl], kt[:, sl], (((0,), (0,)), ((), ())),
                                                         preferred_element_type=F32)
        return carry

    lax.fori_loop(0, n_chunks, chunk, 0, unroll=max(u for u in (1, 2, 3) if n_chunks % u == 0))

    @pl.when(tb == pl.num_programs(1) - 1)
    def _():
        sout_ref[0] = s_ref[...]


def _hgrn(proj, first_block, width, log_lb, log_1m_lb, norm_g, s0_t, tb_rows):
    b, t, _ = proj.shape
    n_heads = width // HEAD_DIM
    assert t % tb_rows == 0 and tb_rows % HG_CHUNK == 0
    col = lambda off: pl.BlockSpec((1, tb_rows, width), lambda bi, ti: (bi, ti, first_block + off))
    vec = pl.BlockSpec((1, width), lambda bi, ti: (0, 0))
    state_spec = pl.BlockSpec((1, n_heads, HEAD_DIM, HEAD_DIM), lambda bi, ti: (bi, 0, 0, 0))
    return pl.pallas_call(
        functools.partial(_hgrn_kernel, n_heads=n_heads, n_chunks=tb_rows // HG_CHUNK),
        out_shape=(jax.ShapeDtypeStruct((b, t, width), BF16),
                   jax.ShapeDtypeStruct((b, n_heads, HEAD_DIM, HEAD_DIM), F32)),
        grid=(b, t // tb_rows),
        in_specs=[col(0), col(1), col(2), col(3), vec, vec,
                  pl.BlockSpec((1, HEAD_DIM), lambda bi, ti: (0, 0)),
                  pl.BlockSpec((HEAD_DIM, HEAD_DIM), lambda bi, ti: (0, 0)), state_spec],
        out_specs=(pl.BlockSpec((1, tb_rows, width), lambda bi, ti: (bi, ti, 0)), state_spec),
        scratch_shapes=[pltpu.VMEM((n_heads, HEAD_DIM, HEAD_DIM), F32), pltpu.VMEM((HG_CHUNK, width), F32),
                        pltpu.VMEM((HG_CHUNK, width), F32)],
        compiler_params=_params("parallel", "arbitrary"),
        name="hgrn",
    )(proj, proj, proj, proj, log_lb.reshape(1, width), log_1m_lb.reshape(1, width),
      norm_g.reshape(1, HEAD_DIM).astype(F32), jnp.ones((HEAD_DIM, HEAD_DIM), BF16), s0_t)


def _layer(x, batch, seq, lw, layer, sb_cache, pool_hist, n_hist, s0, conv_hist):
    rows, d = x.shape
    sb_width = d // 4
    pool_width = d // 4
    hg_width = d // 2
    n_sb_heads = sb_width // HEAD_DIM
    tm = _row_tile(seq, MAX_ROW_TILE) if conv_hist is None else rows
    assert rows % tm == 0

    h = _rmsnorm(x, lw["norm1_g"], BF16)
    proj = _matmul(h, lw["w_in"], layer, tm)
    proj3 = proj.reshape(batch, seq, 3 * d)
    if sb_cache is None:
        a_out = _sb_prompt(proj3, n_sb_heads)
    else:
        a_out = _sb_sample(proj3, sb_cache[0], sb_cache[1], layer, n_sb_heads)
    seq_tile = _row_tile(seq, MAX_ROW_TILE)
    b_out = _pool(proj3, 3, pool_width, pool_hist, n_hist, lw["pool_w"], lw["pool_scale"], seq_tile)
    c_out, s_new_t = _hgrn(proj3, 2, hg_width, lw["log_lb"], lw["log_1m_lb"], lw["hgrn_norm_g"],
                           jnp.swapaxes(s0, -1, -2), _row_tile(seq, MAX_HGRN_ROWS))
    x = _outproj(a_out.reshape(rows, sb_width), b_out.reshape(rows, pool_width),
                 c_out.reshape(rows, hg_width), lw["w_out"], layer, x, tm)

    h2 = _rmsnorm(x, lw["norm2_g"], BF16)
    act = _gateup(h2, lw["w_gate"], lw["w_up"], layer, lw["conv_w"], lw["conv_b"], tm, seq, conv_hist)
    x = _down(act, lw["w_down"], layer, x, tm)

    h2_last = h2.reshape(batch, seq, d)[:, seq - (CONV_W - 1):].reshape(batch * (CONV_W - 1), d)
    n_last = h2_last.shape[0]
    n_last_pad = -(-n_last // BF16_SUBLANES) * BF16_SUBLANES
    h2_last = jnp.pad(h2_last, ((0, n_last_pad - n_last), (0, 0)))
    conv_state = _matmul(h2_last, lw["w_gate"], layer, n_last_pad)[:n_last].reshape(batch, CONV_W - 1, -1)

    k_new = proj3[:, :, sb_width:2 * sb_width].reshape(batch, seq, n_sb_heads, HEAD_DIM)
    v_new = proj3[:, :, 2 * sb_width:3 * sb_width].reshape(batch, seq, n_sb_heads, HEAD_DIM)
    xb = proj3[:, :, 3 * sb_width:3 * sb_width + pool_width]
    pool_new = jnp.concatenate([pool_hist, xb], axis=1)[:, -POOL_HIST:]
    s_new = jnp.swapaxes(s_new_t, -1, -2)
    return x, (k_new, v_new, s_new, pool_new, conv_state)


def kernel(x_prompt, x_sample, cache_sb_k, cache_sb_v, state_hgrn, state_pool, state_conv, meta_tokens,
           norm1_g, w_in, pool_w, pool_scale, hgrn_lower_bounds, hgrn_norm_g, w_out, norm2_g, ffn_w_gate,
           ffn_w_up, ffn_conv_w, ffn_conv_b, ffn_w_down, final_norm_g):
    bp, seq_p, d = x_prompt.shape
    bs, seq_s, _ = x_sample.shape
    depth = w_in.shape[0]
    tp = N_META + seq_p
    hg_heads = (d // 2) // HEAD_DIM

    meta = jnp.broadcast_to(meta_tokens[None], (bp, N_META, d))
    xp = jnp.concatenate([meta, x_prompt], axis=1).reshape(bp * tp, d)
    xs = x_sample.reshape(bs * seq_s, d)

    probs = jax.nn.softmax(hgrn_lower_bounds.astype(F32), axis=0)
    lower = jnp.maximum(jnp.cumsum(probs, axis=0) - probs[0], 0.0)
    log_lb = jnp.log(lower)
    log_1m_lb = jnp.log1p(-lower)

    w_in_b, w_out_b = w_in.astype(BF16), w_out.astype(BF16)
    w_gate_b, w_up_b, w_down_b = ffn_w_gate.astype(BF16), ffn_w_up.astype(BF16), ffn_w_down.astype(BF16)
    pool_w_b = pool_w.astype(BF16)
    cache = (cache_sb_k.reshape(depth, bs, -1, d // 4), cache_sb_v.reshape(depth, bs, -1, d // 4))

    outs_p, outs_s = [], []
    for l in range(depth):
        lw = dict(norm1_g=norm1_g[l], w_in=w_in_b, pool_w=pool_w_b[l], pool_scale=pool_scale[l],
                  log_lb=log_lb[l], log_1m_lb=log_1m_lb[l], hgrn_norm_g=hgrn_norm_g[l], w_out=w_out_b,
                  norm2_g=norm2_g[l], w_gate=w_gate_b, w_up=w_up_b, conv_w=ffn_conv_w[l], conv_b=ffn_conv_b[l],
                  w_down=w_down_b)
        xp, out = _layer(xp, bp, tp, lw, l, None, jnp.zeros((bp, POOL_HIST, d // 4), F32), 0,
                         jnp.zeros((bp, hg_heads, HEAD_DIM, HEAD_DIM), F32), None)
        outs_p.append(out)
        xs, out = _layer(xs, bs, seq_s, lw, l, cache, state_pool[l], POOL_HIST, state_hgrn[l], state_conv[l])
        outs_s.append(out)

    y_prompt = _rmsnorm(xp, final_norm_g, F32).reshape(bp, tp, d)[:, N_META:]
    y_sample = _rmsnorm(xs, final_norm_g, F32).reshape(bs, seq_s, d)
    stack = lambda outs, idx: jnp.stack([o[idx] for o in outs])
    return (y_prompt, y_sample,
            stack(outs_p, 0), stack(outs_p, 1), stack(outs_p, 2), stack(outs_p, 3), stack(outs_p, 4),
            stack(outs_s, 0), stack(outs_s, 1), stack(outs_s, 2), stack(outs_s, 3), stack(outs_s, 4))
```

```python
import functools
import math

import jax
import jax.numpy as jnp
from jax import lax
from jax.experimental import pallas as pl
from jax.experimental.pallas import tpu as pltpu

F32 = jnp.float32
BF16 = jnp.bfloat16

N_META = 16
EPS = 1e-6
HEAD_DIM = 128
POOL_WINDOWS = (2, 4, 8, 16)
POOL_HIST = max(POOL_WINDOWS) - 1
CONV_W = 3

VMEM_LIMIT_BYTES = 56 * 1024 * 1024
BF16_SUBLANES = 16
MAX_ROW_TILE = 1024
MAX_NORM_ROWS = 512
MAX_HGRN_ROWS = 256
SB_BLOCK = 256
SB_UNROLL = 6
HG_CHUNK = 16
COL_TILE = 512
LANE = 128
SIGN_BIT = -2 ** 31
LOG2_E = 1.4426950408889634


def _row_tile(rows, limit):
    best = None
    for t in range(BF16_SUBLANES, min(rows, limit) + 1, BF16_SUBLANES):
        if rows % t == 0:
            best = t
    assert best is not None, rows
    return best


def _params(*semantics):
    return pltpu.CompilerParams(dimension_semantics=semantics, vmem_limit_bytes=VMEM_LIMIT_BYTES)


def _sigmoid(x):
    return 1.0 / (1.0 + jnp.exp(-x))


def _rmsnorm_kernel(x_ref, g_ref, o_ref):
    x = x_ref[...]
    ms = jnp.mean(x * x, axis=-1, keepdims=True)
    o_ref[...] = (x * lax.rsqrt(ms + EPS) * g_ref[...]).astype(o_ref.dtype)


def _rmsnorm_skip(x, g, skip, out_dtype):
    b, t, d = x.shape
    tr = _row_tile(t - skip, MAX_NORM_ROWS)
    assert skip % 8 == 0
    return pl.pallas_call(
        _rmsnorm_kernel,
        out_shape=jax.ShapeDtypeStruct((b, t - skip, d), out_dtype),
        grid=(b, (t - skip) // tr),
        in_specs=[pl.BlockSpec((pl.Element(1), pl.Element(tr), pl.Element(d)),
                               lambda bi, i: (bi, pl.multiple_of(skip + i * tr, 8), 0)),
                  pl.BlockSpec((1, 1, d), lambda bi, i: (0, 0, 0))],
        out_specs=pl.BlockSpec((1, tr, d), lambda bi, i: (bi, i, 0)),
        compiler_params=_params("parallel", "parallel"),
        name="rmsnorm_skip",
    )(x, g.reshape(1, 1, d).astype(F32))


def _rmsnorm(x, g, out_dtype):
    rows, d = x.shape
    tr = _row_tile(rows, MAX_NORM_ROWS)
    return pl.pallas_call(
        _rmsnorm_kernel,
        out_shape=jax.ShapeDtypeStruct((rows, d), out_dtype),
        grid=(rows // tr,),
        in_specs=[pl.BlockSpec((tr, d), lambda i: (i, 0)),
                  pl.BlockSpec((1, d), lambda i: (0, 0))],
        out_specs=pl.BlockSpec((tr, d), lambda i: (i, 0)),
        compiler_params=_params("parallel"),
        name="rmsnorm",
    )(x, g.reshape(1, d).astype(F32))


def _matmul_kernel(x_ref, w_ref, o_ref):
    o_ref[...] = jnp.dot(x_ref[...], w_ref[...], preferred_element_type=F32)


def _matmul(x, w, layer, tm):
    rows, k = x.shape
    n = w.shape[2]
    tn = min(2 * COL_TILE, n)
    return pl.pallas_call(
        _matmul_kernel,
        out_shape=jax.ShapeDtypeStruct((rows, n), F32),
        grid=(pl.cdiv(n, tn), rows // tm),
        in_specs=[pl.BlockSpec((tm, k), lambda j, i: (i, 0)),
                  pl.BlockSpec((None, k, tn), lambda j, i: (layer, 0, j))],
        out_specs=pl.BlockSpec((tm, tn), lambda j, i: (i, j)),
        compiler_params=_params("parallel", "parallel"),
        name="matmul",
    )(x, w)


def _outproj_kernel(a_ref, b_ref, c_ref, wa_ref, wb_ref, wc_ref, x_ref, o_ref):
    acc = jnp.dot(a_ref[...], wa_ref[...], preferred_element_type=F32)
    acc += jnp.dot(b_ref[...], wb_ref[...], preferred_element_type=F32)
    acc += jnp.dot(c_ref[...], wc_ref[...], preferred_element_type=F32)
    o_ref[...] = x_ref[...] + acc


def _outproj(a, b, c, w_out, layer, x, tm):
    rows, d = x.shape
    wa, wb, wc = a.shape[1], b.shape[1], c.shape[1]
    assert wa == wb and wc == 2 * wa and w_out.shape[1] == wa + wb + wc
    tn = min(2 * COL_TILE, d)
    return pl.pallas_call(
        _outproj_kernel,
        out_shape=jax.ShapeDtypeStruct((rows, d), F32),
        grid=(d // tn, rows // tm),
        in_specs=[pl.BlockSpec((tm, wa), lambda j, i: (i, 0)),
                  pl.BlockSpec((tm, wb), lambda j, i: (i, 0)),
                  pl.BlockSpec((tm, wc), lambda j, i: (i, 0)),
                  pl.BlockSpec((None, wa, tn), lambda j, i: (layer, 0, j)),
                  pl.BlockSpec((None, wb, tn), lambda j, i: (layer, 1, j)),
                  pl.BlockSpec((None, wc, tn), lambda j, i: (layer, 1, j)),
                  pl.BlockSpec((tm, tn), lambda j, i: (i, j))],
        out_specs=pl.BlockSpec((tm, tn), lambda j, i: (i, j)),
        compiler_params=_params("parallel", "parallel"),
        name="outproj",
    )(a, b, c, w_out, w_out, w_out, x)


def _ffn_act(g, g1, g2, u, cw_ref, cb_ref):
    conv = cb_ref[...] + g2 * cw_ref[0:1, :] + g1 * cw_ref[1:2, :] + g * cw_ref[2:3, :]
    return (conv * _sigmoid(conv) * u).astype(BF16)


def _gateup_carry_kernel(h_ref, wg_ref, wu_ref, cw_ref, cb_ref, o_ref, tail_ref, g_ref, *, tiles_per_seq):
    i = pl.program_id(1)
    tm = h_ref.shape[0]

    @pl.when(i % tiles_per_seq == 0)
    def _():
        g_ref[0:8, :] = jnp.zeros((8, g_ref.shape[1]), F32)

    @pl.when(i % tiles_per_seq != 0)
    def _():
        g_ref[0:8, :] = g_ref[tm:tm + 8, :]

    h = h_ref[...]
    g_ref[8:tm + 8, :] = jnp.dot(h, wg_ref[...], preferred_element_type=F32)
    u = jnp.dot(h, wu_ref[...], preferred_element_type=F32)
    o_ref[...] = _ffn_act(g_ref[8:tm + 8, :], g_ref[7:tm + 7, :], g_ref[6:tm + 6, :], u, cw_ref, cb_ref)
    tail_ref[0] = g_ref[tm:tm + 8, :]


def _gateup_hist_kernel(h_ref, wg_ref, wu_ref, cw_ref, cb_ref, h1_ref, h2_ref, o_ref, tail_ref, *, seq_len):
    h = h_ref[...]
    g = jnp.dot(h, wg_ref[...], preferred_element_type=F32)
    u = jnp.dot(h, wu_ref[...], preferred_element_type=F32)
    pos = lax.broadcasted_iota(jnp.int32, g.shape, 0) % seq_len
    g1 = jnp.where(pos == 0, h1_ref[...], pltpu.roll(g, 1, 0))
    g2 = jnp.where(pos < 2, h2_ref[...], pltpu.roll(g, 2, 0))
    o_ref[...] = _ffn_act(g, g1, g2, u, cw_ref, cb_ref)
    tail_ref[...] = g.reshape(g.shape[0] // seq_len, seq_len, g.shape[1])[:, seq_len - 8:, :]


def _gateup(h, w_gate, w_up, layer, conv_w, conv_b, tm, seq_len, hist):
    rows, d = h.shape
    n = w_gate.shape[2]
    tn = min(COL_TILE, n)
    grid = (pl.cdiv(n, tn), rows // tm)
    row_spec = pl.BlockSpec((tm, d), lambda j, i: (i, 0))
    w_spec = pl.BlockSpec((None, d, tn), lambda j, i: (layer, 0, j))
    cw_spec = pl.BlockSpec((CONV_W, tn), lambda j, i: (0, j))
    cb_spec = pl.BlockSpec((1, tn), lambda j, i: (0, j))
    out_spec = pl.BlockSpec((tm, tn), lambda j, i: (i, j))
    out_shape = (jax.ShapeDtypeStruct((rows, n), BF16), jax.ShapeDtypeStruct((rows // seq_len, 8, n), F32))
    cb = conv_b.reshape(1, n)
    if hist is None:
        assert seq_len % tm == 0
        return pl.pallas_call(
            functools.partial(_gateup_carry_kernel, tiles_per_seq=seq_len // tm),
            out_shape=out_shape, grid=grid,
            in_specs=[row_spec, w_spec, w_spec, cw_spec, cb_spec],
            out_specs=(out_spec, pl.BlockSpec((1, 8, tn), lambda j, i: (i // (seq_len // tm), 0, j))),
            scratch_shapes=[pltpu.VMEM((tm + 8, tn), F32)],
            compiler_params=_params("arbitrary", "arbitrary"),
            name="gateup_carry",
        )(h, w_gate, w_up, conv_w, cb)
    assert tm % seq_len == 0
    nb = rows // seq_len
    zeros = jnp.zeros((nb, seq_len, n), F32)
    h1 = zeros.at[:, 0].set(hist[:, 1]).reshape(rows, n)
    h2 = zeros.at[:, 0].set(hist[:, 0]).at[:, 1].set(hist[:, 1]).reshape(rows, n)
    hist_spec = pl.BlockSpec((tm, tn), lambda j, i: (i, j))
    return pl.pallas_call(
        functools.partial(_gateup_hist_kernel, seq_len=seq_len),
        out_shape=out_shape, grid=grid,
        in_specs=[row_spec, w_spec, w_spec, cw_spec, cb_spec, hist_spec, hist_spec],
        out_specs=(out_spec, pl.BlockSpec((tm // seq_len, 8, tn), lambda j, i: (i, 0, j))),
        compiler_params=_params("parallel", "parallel"),
        name="gateup_hist",
    )(h, w_gate, w_up, conv_w, cb, h1, h2)


def _down_kernel(a_ref, w_ref, x_ref, o_ref):
    o_ref[...] = x_ref[...] + jnp.dot(a_ref[...], w_ref[...], preferred_element_type=F32)


def _down(act, w_down, layer, x, tm):
    rows, n = act.shape
    d = x.shape[1]
    tn = min(COL_TILE // 2, d)
    return pl.pallas_call(
        _down_kernel,
        out_shape=jax.ShapeDtypeStruct((rows, d), F32),
        grid=(rows // tm, d // tn),
        in_specs=[pl.BlockSpec((tm, n), lambda i, j: (i, 0)),
                  pl.BlockSpec((None, n, tn), lambda i, j: (layer, 0, j)),
                  pl.BlockSpec((tm, tn), lambda i, j: (i, j))],
        out_specs=pl.BlockSpec((tm, tn), lambda i, j: (i, j)),
        compiler_params=_params("parallel", "arbitrary"),
        name="down",
    )(act, w_down, x)


def _sb_logs(z, mask):
    neg_abs = lax.bitcast_convert_type(lax.bitcast_convert_type(z, jnp.int32) | SIGN_BIT, F32)
    sp = jnp.maximum(z, 0.0) + jnp.log2(1.0 + jnp.exp2(neg_abs))
    return z - sp, (sp if mask is None else jnp.where(mask, sp, 0.0))


def _sb_split(drop):
    hi = drop.astype(BF16)
    return hi, (drop - hi.astype(F32)).astype(BF16)


def _sb_prompt_kernel(aq_ref, aj_ref, bq_ref, dq_ref, dj_ref, q_ref, k_ref, v_ref, u_ref, o_ref,
                      qb, kb, vb, acc, car, z_s, beta_s, hi_s, lo_s, w_s,
                      *, scale, t_len, n_steps):
    blk = SB_BLOCK
    t_pad = kb.shape[0]
    nq = t_pad // blk
    nt = (((1,), (1,)), ((), ()))

    for src, dst, mul in ((q_ref, qb, scale), (k_ref, kb, None), (v_ref, vb, None)):
        x = src[0] if mul is None else src[0] * mul
        dst[0:t_len, :] = x.astype(BF16)
        if t_pad > t_len:
            dst[t_len:t_pad, :] = jnp.zeros((t_pad - t_len, HEAD_DIM), BF16)
    for ref in (z_s, beta_s, hi_s, lo_s, w_s):
        ref[...] = jnp.zeros_like(ref)

    row = lax.broadcasted_iota(jnp.int32, (blk, blk), 0)
    col = lax.broadcasted_iota(jnp.int32, (blk, blk), 1)
    causal = col < row

    def rows(i):
        return pl.ds(pl.multiple_of(i * blk, blk), blk)

    def cumsum(hi, lo):
        return (jnp.dot(hi, u_ref[...], preferred_element_type=F32)
                + jnp.dot(lo, u_ref[...], preferred_element_type=F32))

    def diag(blocks):
        rs = [rows(qi) for qi in blocks]
        logs = [_sb_logs(lax.dot_general(qb[r, :], kb[r, :], nt, preferred_element_type=F32), causal) for r in rs]
        gaps = [cumsum(*_sb_split(drop)) for _, drop in logs]
        for qi, r, (beta, drop), gap in zip(blocks, rs, logs, gaps):
            w = jnp.where(causal, jnp.exp2(beta - gap), 0.0)
            acc[qi] = jnp.dot(w.astype(BF16), vb[r, :], preferred_element_type=F32)
            car[qi] = jnp.sum(drop, axis=-1, keepdims=True)

    def diag_pair(i, c):
        diag([2 * i, 2 * i + 1])
        return c

    lax.fori_loop(0, nq // 2, diag_pair, 0)
    if nq % 2:
        diag([nq - 1])

    def step(t, slot):
        other = 1 - slot
        acc[dq_ref[t]] += jnp.dot(w_s[other], vb[rows(dj_ref[t]), :], preferred_element_type=F32)
        w_s[slot] = jnp.exp2(beta_s[slot] - cumsum(hi_s[slot], lo_s[slot])).astype(BF16)
        z_s[slot] = lax.dot_general(qb[rows(aq_ref[t]), :], kb[rows(aj_ref[t]), :], nt,
                                    preferred_element_type=F32)
        q1 = bq_ref[t]
        beta, drop = _sb_logs(z_s[other], None)
        carry = car[q1]
        beta_s[other] = beta - carry
        hi_s[other], lo_s[other] = _sb_split(drop)
        car[q1] = carry + jnp.sum(drop, axis=-1, keepdims=True)

    def steps(it, c):
        for s in range(SB_UNROLL):
            step(it * SB_UNROLL + s, s % 2)
        return c

    acc[nq] = jnp.zeros((blk, HEAD_DIM), F32)
    car[nq] = jnp.zeros((blk, 1), F32)
    lax.fori_loop(0, n_steps // SB_UNROLL, steps, 0)
    for qi in range(nq):
        r = slice(qi * blk, min((qi + 1) * blk, t_len))
        o_ref[0, r, :] = acc[qi, 0:r.stop - r.start, :].astype(o_ref.dtype)


def _tri_ones(n):
    r = lax.broadcasted_iota(jnp.int32, (n, n), 0)
    c = lax.broadcasted_iota(jnp.int32, (n, n), 1)
    return (r > c).astype(BF16)


def _sb_prompt(proj, n_heads):
    b, t, _ = proj.shape
    blk = SB_BLOCK
    nq = pl.cdiv(t, blk)
    t_pad = nq * blk
    scale = LOG2_E / math.sqrt(HEAD_DIM)
    items = [(qi, j) for qi in range(1, nq) for j in range(qi - 1, -1, -1)]
    n_steps = 0 if not items else -(-(len(items) + 3) // SB_UNROLL) * SB_UNROLL
    item = lambda m: items[m] if 0 <= m < len(items) else None
    table = lambda delay, pick, spare: jnp.asarray(
        [spare if item(s - delay) is None else item(s - delay)[pick] for s in range(n_steps)] + [spare], jnp.int32)
    tables = (table(0, 0, 0), table(0, 1, 0), table(1, 0, nq), table(3, 0, nq), table(3, 1, 0))
    head = lambda off: pl.BlockSpec((1, t, HEAD_DIM), lambda bi, h, *_: (bi, 0, off + h))
    ring = lambda dt: pltpu.VMEM((2, blk, blk), dt)
    return pl.pallas_call(
        functools.partial(_sb_prompt_kernel, scale=scale, t_len=t, n_steps=n_steps),
        out_shape=jax.ShapeDtypeStruct((b, t, n_heads * HEAD_DIM), BF16),
        grid_spec=pltpu.PrefetchScalarGridSpec(
            num_scalar_prefetch=5,
            grid=(b, n_heads),
            in_specs=[head(0), head(n_heads), head(2 * n_heads),
                      pl.BlockSpec((blk, blk), lambda bi, h, *_: (0, 0))],
            out_specs=pl.BlockSpec((1, t, HEAD_DIM), lambda bi, h, *_: (bi, 0, h)),
            scratch_shapes=[pltpu.VMEM((t_pad, HEAD_DIM), BF16)] * 3
                           + [pltpu.VMEM((nq + 1, blk, HEAD_DIM), F32), pltpu.VMEM((nq + 1, blk, 1), F32),
                              ring(F32), ring(F32), ring(BF16), ring(BF16), ring(BF16)]),
        compiler_params=_params("parallel", "parallel"),
        name="sb_prompt",
    )(*tables, proj, proj, proj, _tri_ones(blk))


def _sb_sample_kernel(q_ref, kn_ref, vn_ref, kc_ref, vc_ref, u_ref, o_ref, *, scale, n_cache_blocks):
    blk = SB_BLOCK
    n = q_ref.shape[1]
    nt = (((1,), (1,)), ((), ()))
    q = (q_ref[0] * scale).astype(BF16)
    row = lax.broadcasted_iota(jnp.int32, (n, n), 0)
    col = lax.broadcasted_iota(jnp.int32, (n, n), 1)
    keys = [kn_ref[0].astype(BF16)] + [kc_ref[0, j * blk:(j + 1) * blk, :].astype(BF16)
                                       for j in reversed(range(n_cache_blocks))]
    vals = [vn_ref[0].astype(BF16)] + [vc_ref[0, j * blk:(j + 1) * blk, :].astype(BF16)
                                       for j in reversed(range(n_cache_blocks))]
    masks = [col < row] + [None] * n_cache_blocks
    tris = [u_ref[0:n, 0:n]] + [u_ref[...]] * n_cache_blocks
    logs = [_sb_logs(lax.dot_general(q, k, nt, preferred_element_type=F32), m) for k, m in zip(keys, masks)]
    gaps = []
    for (_, drop), u in zip(logs, tris):
        hi, lo = _sb_split(drop)
        gaps.append(jnp.dot(hi, u, preferred_element_type=F32) + jnp.dot(lo, u, preferred_element_type=F32))
    carry = jnp.zeros((n, 1), F32)
    acc = jnp.zeros((n, HEAD_DIM), F32)
    for (beta, drop), gap, v, m in zip(logs, gaps, vals, masks):
        w = jnp.exp2(beta - gap - carry)
        if m is not None:
            w = jnp.where(m, w, 0.0)
        acc = acc + jnp.dot(w.astype(BF16), v, preferred_element_type=F32)
        carry = carry + jnp.sum(drop, axis=-1, keepdims=True)
    o_ref[0] = acc.astype(o_ref.dtype)


def _sb_sample(proj, cache_k, cache_v, layer, n_heads):
    b, n, _ = proj.shape
    p = cache_k.shape[2]
    blk = SB_BLOCK
    assert p % blk == 0 and n <= blk
    scale = LOG2_E / math.sqrt(HEAD_DIM)
    new_spec = lambda off: pl.BlockSpec((1, n, HEAD_DIM), lambda bi, h: (bi, 0, off + h))
    cache_spec = pl.BlockSpec((None, 1, p, HEAD_DIM), lambda bi, h: (layer, bi, 0, h))
    return pl.pallas_call(
        functools.partial(_sb_sample_kernel, scale=scale, n_cache_blocks=p // blk),
        out_shape=jax.ShapeDtypeStruct((b, n, n_heads * HEAD_DIM), BF16),
        grid=(b, n_heads),
        in_specs=[new_spec(0), new_spec(n_heads), new_spec(2 * n_heads), cache_spec, cache_spec,
                  pl.BlockSpec((blk, blk), lambda bi, h: (0, 0))],
        out_specs=pl.BlockSpec((1, n, HEAD_DIM), lambda bi, h: (bi, 0, h)),
        compiler_params=_params("parallel", "parallel"),
        name="sb_sample",
    )(proj, proj, proj, cache_k, cache_v, _tri_ones(blk))


def _pool_kernel(x_ref, hist_ref, w_ref, scale_ref, o_ref, ext_ref, *, n_hist, group_dim):
    ti = pl.program_id(1)
    tm = x_ref.shape[1]
    pad = POOL_HIST + 1

    @pl.when(ti == 0)
    def _():
        ext_ref[0:pad, :] = hist_ref[0]

    @pl.when(ti != 0)
    def _():
        ext_ref[0:pad, :] = ext_ref[tm:tm + pad, :]

    ext_ref[pad:pad + tm, :] = x_ref[0]
    t = ti * tm + lax.broadcasted_iota(jnp.int32, (tm, 1), 0)
    for gi, win in enumerate(POOL_WINDOWS):
        cols = slice(gi * group_dim, (gi + 1) * group_dim)
        x = ext_ref[pad:pad + tm, cols]
        total = x
        for d in range(1, win):
            total = total + ext_ref[pad - d:pad - d + tm, cols]
        count = jnp.minimum(t + 1 + n_hist, win).astype(F32)
        pooled = total / count - x
        y = jnp.dot(pooled.astype(BF16), w_ref[gi], preferred_element_type=F32)
        o_ref[0, :, cols] = (y * scale_ref[:, cols]).astype(o_ref.dtype)


def _pool(proj, col_block, width, hist, n_hist, w_pool, scale, tm):
    b, t, _ = proj.shape
    groups = len(POOL_WINDOWS)
    group_dim = width // groups
    pad = POOL_HIST + 1
    hist_pad = jnp.concatenate([jnp.zeros((b, 1, width), F32), hist], axis=1)
    return pl.pallas_call(
        functools.partial(_pool_kernel, n_hist=n_hist, group_dim=group_dim),
        out_shape=jax.ShapeDtypeStruct((b, t, width), BF16),
        grid=(b, t // tm),
        in_specs=[pl.BlockSpec((1, tm, width), lambda bi, ti: (bi, ti, col_block)),
                  pl.BlockSpec((1, pad, width), lambda bi, ti: (bi, 0, 0)),
                  pl.BlockSpec((groups, group_dim, group_dim), lambda bi, ti: (0, 0, 0)),
                  pl.BlockSpec((1, width), lambda bi, ti: (0, 0))],
        out_specs=pl.BlockSpec((1, tm, width), lambda bi, ti: (bi, ti, 0)),
        scratch_shapes=[pltpu.VMEM((tm + pad, width), F32)],
        compiler_params=_params("parallel", "arbitrary"),
        name="pool",
    )(proj, hist_pad, w_pool, scale.reshape(1, width))


def _cumsum_rows(x):
    row = lax.broadcasted_iota(jnp.int32, x.shape, 0)
    shift = 1
    while shift < x.shape[0]:
        x = x + jnp.where(row >= shift, pltpu.roll(x, shift, 0), 0.0)
        shift *= 2
    return x


def _hgrn_kernel(q_ref, f_ref, i_ref, g_ref, la_ref, lc_ref, ng_ref, ones_ref, s0_ref, o_ref, sout_ref,
                 s_ref, ck_ref, vs_ref, *, n_heads, n_chunks):
    tb = pl.program_id(1)
    half = HG_CHUNK // 2
    hd = HEAD_DIM

    @pl.when(tb == 0)
    def _():
        s_ref[...] = s0_ref[0]

    la = la_ref[...]
    lc = lc_ref[...]
    ng = ng_ref[...]
    row8 = lax.broadcasted_iota(jnp.int32, (half, hd), 0)

    def chunk(c, carry):
        r0 = pl.multiple_of(c * HG_CHUNK, HG_CHUNK)
        qc = q_ref[0, pl.ds(r0, HG_CHUNK), :]
        fc = f_ref[0, pl.ds(r0, HG_CHUNK), :]
        gc = g_ref[0, pl.ds(r0, HG_CHUNK), :]
        q = qc * _sigmoid(qc)
        gate = gc * _sigmoid(gc)
        l1 = jnp.log(1.0 + jnp.exp(-jnp.abs(fc)))
        log_sig = jnp.minimum(fc, 0.0) - l1
        log_sig_neg = jnp.minimum(-fc, 0.0) - l1
        b = lc + log_sig
        log_f = jnp.maximum(la, b) + jnp.log(1.0 + jnp.exp(-jnp.abs(la - b)))
        log2_k = (lc + log_sig_neg) * LOG2_E
        cum = _cumsum_rows(log_f * LOG2_E)
        last = cum[HG_CHUNK - 1:HG_CHUNK, :]
        qt = (q * jnp.exp2(cum)).astype(BF16)
        kt = jnp.exp2(log2_k + last - cum).astype(BF16)
        dec = jnp.exp2(last)
        ck_ref[...] = cum - log2_k
        v = i_ref[0, pl.ds(r0, HG_CHUNK), :]
        vs_ref[...] = v
        vb = v.astype(BF16)
        for h in range(n_heads):
            sl = slice(h * hd, (h + 1) * hd)
            cum_a, cum_b = cum[0:half, sl], cum[half:, sl]
            q_a, q_b = q[0:half, sl], q[half:, sl]
            pairs = []
            for s in range(HG_CHUNK):
                cs = jnp.broadcast_to(ck_ref[s:s + 1, sl], (half, hd))
                if s < half:
                    pairs.append(q_a * jnp.where(row8 >= s, jnp.exp2(cum_a - cs), 0.0))
                    pairs.append(q_b * jnp.exp2(cum_b - cs))
                else:
                    pairs.append(q_b * jnp.where(row8 + half >= s, jnp.exp2(cum_b - cs), 0.0))
            scores = jnp.dot(jnp.concatenate(pairs, axis=0).astype(BF16), ones_ref[...],
                             preferred_element_type=F32)
            o_a = jnp.zeros((half, hd), F32)
            o_b = jnp.zeros((half, hd), F32)
            blocks = iter(range(len(pairs)))
            for s in range(HG_CHUNK):
                vs = jnp.broadcast_to(vs_ref[s:s + 1, sl], (half, hd))
                if s < half:
                    i = next(blocks)
                    o_a = o_a + scores[i * half:(i + 1) * half] * vs
                i = next(blocks)
                o_b = o_b + scores[i * half:(i + 1) * half] * vs
            st = s_ref[h]
            inter = lax.dot_general(qt[:, sl], st.astype(BF16), (((1,), (1,)), ((), ())),
                                    preferred_element_type=F32)
            o = jnp.concatenate([o_a, o_b], axis=0) + inter
            ms = jnp.mean(o * o, axis=-1, keepdims=True)
            o = o * lax.rsqrt(ms + EPS) * ng * gate[:, sl]
            o_ref[0, pl.ds(r0, HG_CHUNK), sl] = o.astype(o_ref.dtype)
            s_ref[h] = st * dec[:, sl] + lax.dot_general(vb[:, sl], kt[:, sl], (((0,), (0,)), ((), ())),
                                                         preferred_element_type=F32)
        return carry

    lax.fori_loop(0, n_chunks, chunk, 0, unroll=max(u for u in (1, 2, 3) if n_chunks % u == 0))

    @pl.when(tb == pl.num_programs(1) - 1)
    def _():
        sout_ref[0] = s_ref[...]


def _hgrn(proj, first_block, width, log_lb, log_1m_lb, norm_g, s0_t, tb_rows):
    b, t, _ = proj.shape
    n_heads = width // HEAD_DIM
    assert t % tb_rows == 0 and tb_rows % HG_CHUNK == 0
    col = lambda off: pl.BlockSpec((1, tb_rows, width), lambda bi, ti: (bi, ti, first_block + off))
    vec = pl.BlockSpec((1, width), lambda bi, ti: (0, 0))
    state_spec = pl.BlockSpec((1, n_heads, HEAD_DIM, HEAD_DIM), lambda bi, ti: (bi, 0, 0, 0))
    return pl.pallas_call(
        functools.partial(_hgrn_kernel, n_heads=n_heads, n_chunks=tb_rows // HG_CHUNK),
        out_shape=(jax.ShapeDtypeStruct((b, t, width), BF16),
                   jax.ShapeDtypeStruct((b, n_heads, HEAD_DIM, HEAD_DIM), F32)),
        grid=(b, t // tb_rows),
        in_specs=[col(0), col(1), col(2), col(3), vec, vec,
                  pl.BlockSpec((1, HEAD_DIM), lambda bi, ti: (0, 0)),
                  pl.BlockSpec((HEAD_DIM, HEAD_DIM), lambda bi, ti: (0, 0)), state_spec],
        out_specs=(pl.BlockSpec((1, tb_rows, width), lambda bi, ti: (bi, ti, 0)), state_spec),
        scratch_shapes=[pltpu.VMEM((n_heads, HEAD_DIM, HEAD_DIM), F32), pltpu.VMEM((HG_CHUNK, width), F32),
                        pltpu.VMEM((HG_CHUNK, width), F32)],
        compiler_params=_params("parallel", "arbitrary"),
        name="hgrn",
    )(proj, proj, proj, proj, log_lb.reshape(1, width), log_1m_lb.reshape(1, width),
      norm_g.reshape(1, HEAD_DIM).astype(F32), jnp.ones((HEAD_DIM, HEAD_DIM), BF16), s0_t)


def _layer(x, batch, seq, lw, layer, sb_cache, pool_hist, n_hist, s0, conv_hist):
    rows, d = x.shape
    sb_width = d // 4
    pool_width = d // 4
    hg_width = d // 2
    n_sb_heads = sb_width // HEAD_DIM
    tm = _row_tile(seq, MAX_ROW_TILE) if conv_hist is None else rows
    assert rows % tm == 0

    h = _rmsnorm(x, lw["norm1_g"], BF16)
    proj = _matmul(h, lw["w_in"], layer, tm)
    proj3 = proj.reshape(batch, seq, 3 * d)
    if sb_cache is None:
        a_out = _sb_prompt(proj3, n_sb_heads)
    else:
        a_out = _sb_sample(proj3, sb_cache[0], sb_cache[1], layer, n_sb_heads)
    seq_tile = _row_tile(seq, MAX_ROW_TILE)
    b_out = _pool(proj3, 3, pool_width, pool_hist, n_hist, lw["pool_w"], lw["pool_scale"], seq_tile)
    c_out, s_new_t = _hgrn(proj3, 2, hg_width, lw["log_lb"], lw["log_1m_lb"], lw["hgrn_norm_g"],
                           jnp.swapaxes(s0, -1, -2), _row_tile(seq, MAX_HGRN_ROWS))
    x = _outproj(a_out.reshape(rows, sb_width), b_out.reshape(rows, pool_width),
                 c_out.reshape(rows, hg_width), lw["w_out"], layer, x, tm)

    h2 = _rmsnorm(x, lw["norm2_g"], BF16)
    act, g_tail = _gateup(h2, lw["w_gate"], lw["w_up"], layer, lw["conv_w"], lw["conv_b"], tm, seq, conv_hist)
    x = _down(act, lw["w_down"], layer, x, tm)
    conv_state = g_tail[:, 8 - (CONV_W - 1):, :]

    k_new = proj3[:, :, sb_width:2 * sb_width].reshape(batch, seq, n_sb_heads, HEAD_DIM)
    v_new = proj3[:, :, 2 * sb_width:3 * sb_width].reshape(batch, seq, n_sb_heads, HEAD_DIM)
    xb = proj3[:, :, 3 * sb_width:3 * sb_width + pool_width]
    pool_new = jnp.concatenate([pool_hist, xb], axis=1)[:, -POOL_HIST:]
    s_new = jnp.swapaxes(s_new_t, -1, -2)
    return x, (k_new, v_new, s_new, pool_new, conv_state)


def kernel(x_prompt, x_sample, cache_sb_k, cache_sb_v, state_hgrn, state_pool, state_conv, meta_tokens,
           norm1_g, w_in, pool_w, pool_scale, hgrn_lower_bounds, hgrn_norm_g, w_out, norm2_g, ffn_w_gate,
           ffn_w_up, ffn_conv_w, ffn_conv_b, ffn_w_down, final_norm_g):
    bp, seq_p, d = x_prompt.shape
    bs, seq_s, _ = x_sample.shape
    depth = w_in.shape[0]
    tp = N_META + seq_p
    hg_heads = (d // 2) // HEAD_DIM

    meta = jnp.broadcast_to(meta_tokens[None], (bp, N_META, d))
    xp = jnp.concatenate([meta, x_prompt], axis=1).reshape(bp * tp, d)
    xs = x_sample.reshape(bs * seq_s, d)

    probs = jax.nn.softmax(hgrn_lower_bounds.astype(F32), axis=0)
    lower = jnp.maximum(jnp.cumsum(probs, axis=0) - probs[0], 0.0)
    log_lb = jnp.log(lower)
    log_1m_lb = jnp.log1p(-lower)

    w_in_b, w_out_b = w_in.astype(BF16), w_out.astype(BF16)
    w_gate_b, w_up_b, w_down_b = ffn_w_gate.astype(BF16), ffn_w_up.astype(BF16), ffn_w_down.astype(BF16)
    pool_w_b = pool_w.astype(BF16)
    cache = (cache_sb_k.reshape(depth, bs, -1, d // 4), cache_sb_v.reshape(depth, bs, -1, d // 4))

    outs_p, outs_s = [], []
    for l in range(depth):
        lw = dict(norm1_g=norm1_g[l], w_in=w_in_b, pool_w=pool_w_b[l], pool_scale=pool_scale[l],
                  log_lb=log_lb[l], log_1m_lb=log_1m_lb[l], hgrn_norm_g=hgrn_norm_g[l], w_out=w_out_b,
                  norm2_g=norm2_g[l], w_gate=w_gate_b, w_up=w_up_b, conv_w=ffn_conv_w[l], conv_b=ffn_conv_b[l],
                  w_down=w_down_b)
        xp, out = _layer(xp, bp, tp, lw, l, None, jnp.zeros((bp, POOL_HIST, d // 4), F32), 0,
                         jnp.zeros((bp, hg_heads, HEAD_DIM, HEAD_DIM), F32), None)
        outs_p.append(out)
        xs, out = _layer(xs, bs, seq_s, lw, l, cache, state_pool[l], POOL_HIST, state_hgrn[l], state_conv[l])
        outs_s.append(out)

    y_prompt = _rmsnorm_skip(xp.reshape(bp, tp, d), final_norm_g, N_META, F32)
    y_sample = _rmsnorm(xs, final_norm_g, F32).reshape(bs, seq_s, d)
    stack = lambda outs, idx: jnp.stack([o[idx] for o in outs])
    return (y_prompt, y_sample,
            stack(outs_p, 0), stack(outs_p, 1), stack(outs_p, 2), stack(outs_p, 3), stack(outs_p, 4),
            stack(outs_s, 0), stack(outs_s, 1), stack(outs_s, 2), stack(outs_s, 3), stack(outs_s, 4))
```

```python
import functools
import math

import jax
import jax.numpy as jnp
from jax import lax
from jax.experimental import pallas as pl
from jax.experimental.pallas import tpu as pltpu

F32 = jnp.float32
BF16 = jnp.bfloat16

N_META = 16
EPS = 1e-6
HEAD_DIM = 128
POOL_WINDOWS = (2, 4, 8, 16)
POOL_HIST = max(POOL_WINDOWS) - 1
CONV_W = 3

VMEM_LIMIT_BYTES = 56 * 1024 * 1024
BF16_SUBLANES = 16
MAX_ROW_TILE = 1024
MAX_NORM_ROWS = 512
MAX_HGRN_ROWS = 256
SB_BLOCK = 256
SB_UNROLL = 6
HG_CHUNK = 16
COL_TILE = 512
LANE = 128
SIGN_BIT = -2 ** 31
LOG2_E = 1.4426950408889634


def _row_tile(rows, limit):
    best = None
    for t in range(BF16_SUBLANES, min(rows, limit) + 1, BF16_SUBLANES):
        if rows % t == 0:
            best = t
    assert best is not None, rows
    return best


def _params(*semantics):
    return pltpu.CompilerParams(dimension_semantics=semantics, vmem_limit_bytes=VMEM_LIMIT_BYTES)


def _sigmoid(x):
    return 1.0 / (1.0 + jnp.exp(-x))


def _rmsnorm_kernel(x_ref, g_ref, o_ref):
    x = x_ref[...]
    ms = jnp.mean(x * x, axis=-1, keepdims=True)
    o_ref[...] = (x * lax.rsqrt(ms + EPS) * g_ref[...]).astype(o_ref.dtype)


def _rmsnorm_skip(x, g, skip, out_dtype):
    b, t, d = x.shape
    tr = _row_tile(t - skip, MAX_NORM_ROWS)
    assert skip % 8 == 0
    return pl.pallas_call(
        _rmsnorm_kernel,
        out_shape=jax.ShapeDtypeStruct((b, t - skip, d), out_dtype),
        grid=(b, (t - skip) // tr),
        in_specs=[pl.BlockSpec((pl.Element(1), pl.Element(tr), pl.Element(d)),
                               lambda bi, i: (bi, pl.multiple_of(skip + i * tr, 8), 0)),
                  pl.BlockSpec((1, 1, d), lambda bi, i: (0, 0, 0))],
        out_specs=pl.BlockSpec((1, tr, d), lambda bi, i: (bi, i, 0)),
        compiler_params=_params("parallel", "parallel"),
        name="rmsnorm_skip",
    )(x, g.reshape(1, 1, d).astype(F32))


def _rmsnorm(x, g, out_dtype):
    rows, d = x.shape
    tr = _row_tile(rows, MAX_NORM_ROWS)
    return pl.pallas_call(
        _rmsnorm_kernel,
        out_shape=jax.ShapeDtypeStruct((rows, d), out_dtype),
        grid=(rows // tr,),
        in_specs=[pl.BlockSpec((tr, d), lambda i: (i, 0)),
                  pl.BlockSpec((1, d), lambda i: (0, 0))],
        out_specs=pl.BlockSpec((tr, d), lambda i: (i, 0)),
        compiler_params=_params("parallel"),
        name="rmsnorm",
    )(x, g.reshape(1, d).astype(F32))


def _matmul_kernel(x_ref, w_ref, o_ref):
    o_ref[...] = jnp.dot(x_ref[...], w_ref[...], preferred_element_type=F32)


def _matmul(x, w, layer, tm):
    rows, k = x.shape
    n = w.shape[2]
    tn = min(2 * COL_TILE, n)
    return pl.pallas_call(
        _matmul_kernel,
        out_shape=jax.ShapeDtypeStruct((rows, n), F32),
        grid=(pl.cdiv(n, tn), rows // tm),
        in_specs=[pl.BlockSpec((tm, k), lambda j, i: (i, 0)),
                  pl.BlockSpec((None, k, tn), lambda j, i: (layer, 0, j))],
        out_specs=pl.BlockSpec((tm, tn), lambda j, i: (i, j)),
        compiler_params=_params("parallel", "parallel"),
        name="matmul",
    )(x, w)


def _heads_kernel(x_ref, *refs, n_heads):
    o_ref = refs[-1]
    for h in range(n_heads):
        o_ref[0, 0, :, h, :] = x_ref[0, :, h * HEAD_DIM:(h + 1) * HEAD_DIM]


def _to_heads(proj, col_block, n_heads, stacked, n_layers, layer, tm):
    b, t, _ = proj.shape
    shape = (n_layers, b, t, n_heads, HEAD_DIM)
    prev = () if stacked is None else (stacked,)
    return pl.pallas_call(
        functools.partial(_heads_kernel, n_heads=n_heads),
        out_shape=jax.ShapeDtypeStruct(shape, F32),
        grid=(b, t // tm),
        in_specs=[pl.BlockSpec((1, tm, n_heads * HEAD_DIM), lambda bi, i: (bi, i, col_block))]
                 + [pl.BlockSpec(memory_space=pl.ANY)] * len(prev),
        out_specs=pl.BlockSpec((1, 1, tm, n_heads, HEAD_DIM), lambda bi, i: (layer, bi, i, 0, 0)),
        input_output_aliases={1: 0} if prev else {},
        compiler_params=_params("parallel", "parallel"),
        name="to_heads",
    )(proj, *prev)


def _outproj_kernel(a_ref, b_ref, c_ref, wa_ref, wb_ref, wc_ref, x_ref, o_ref):
    acc = jnp.dot(a_ref[...], wa_ref[...], preferred_element_type=F32)
    acc += jnp.dot(b_ref[...], wb_ref[...], preferred_element_type=F32)
    acc += jnp.dot(c_ref[...], wc_ref[...], preferred_element_type=F32)
    o_ref[...] = x_ref[...] + acc


def _outproj(a, b, c, w_out, layer, x, tm):
    rows, d = x.shape
    wa, wb, wc = a.shape[1], b.shape[1], c.shape[1]
    assert wa == wb and wc == 2 * wa and w_out.shape[1] == wa + wb + wc
    tn = min(2 * COL_TILE, d)
    return pl.pallas_call(
        _outproj_kernel,
        out_shape=jax.ShapeDtypeStruct((rows, d), F32),
        grid=(d // tn, rows // tm),
        in_specs=[pl.BlockSpec((tm, wa), lambda j, i: (i, 0)),
                  pl.BlockSpec((tm, wb), lambda j, i: (i, 0)),
                  pl.BlockSpec((tm, wc), lambda j, i: (i, 0)),
                  pl.BlockSpec((None, wa, tn), lambda j, i: (layer, 0, j)),
                  pl.BlockSpec((None, wb, tn), lambda j, i: (layer, 1, j)),
                  pl.BlockSpec((None, wc, tn), lambda j, i: (layer, 1, j)),
                  pl.BlockSpec((tm, tn), lambda j, i: (i, j))],
        out_specs=pl.BlockSpec((tm, tn), lambda j, i: (i, j)),
        compiler_params=_params("parallel", "parallel"),
        name="outproj",
    )(a, b, c, w_out, w_out, w_out, x)


def _ffn_act(g, g1, g2, u, cw_ref, cb_ref):
    conv = cb_ref[...] + g2 * cw_ref[0:1, :] + g1 * cw_ref[1:2, :] + g * cw_ref[2:3, :]
    return (conv * _sigmoid(conv) * u).astype(BF16)


def _gateup_carry_kernel(h_ref, wg_ref, wu_ref, cw_ref, cb_ref, o_ref, tail_ref, g_ref, *, tiles_per_seq):
    i = pl.program_id(1)
    tm = h_ref.shape[0]

    @pl.when(i % tiles_per_seq == 0)
    def _():
        g_ref[0:8, :] = jnp.zeros((8, g_ref.shape[1]), F32)

    @pl.when(i % tiles_per_seq != 0)
    def _():
        g_ref[0:8, :] = g_ref[tm:tm + 8, :]

    h = h_ref[...]
    g_ref[8:tm + 8, :] = jnp.dot(h, wg_ref[...], preferred_element_type=F32)
    u = jnp.dot(h, wu_ref[...], preferred_element_type=F32)
    o_ref[...] = _ffn_act(g_ref[8:tm + 8, :], g_ref[7:tm + 7, :], g_ref[6:tm + 6, :], u, cw_ref, cb_ref)
    tail_ref[0] = g_ref[tm:tm + 8, :]


def _gateup_hist_kernel(h_ref, wg_ref, wu_ref, cw_ref, cb_ref, h1_ref, h2_ref, o_ref, tail_ref, *, seq_len):
    h = h_ref[...]
    g = jnp.dot(h, wg_ref[...], preferred_element_type=F32)
    u = jnp.dot(h, wu_ref[...], preferred_element_type=F32)
    pos = lax.broadcasted_iota(jnp.int32, g.shape, 0) % seq_len
    g1 = jnp.where(pos == 0, h1_ref[...], pltpu.roll(g, 1, 0))
    g2 = jnp.where(pos < 2, h2_ref[...], pltpu.roll(g, 2, 0))
    o_ref[...] = _ffn_act(g, g1, g2, u, cw_ref, cb_ref)
    tail_ref[...] = g.reshape(g.shape[0] // seq_len, seq_len, g.shape[1])[:, seq_len - 8:, :]


def _gateup(h, w_gate, w_up, layer, conv_w, conv_b, tm, seq_len, hist):
    rows, d = h.shape
    n = w_gate.shape[2]
    tn = min(COL_TILE, n)
    grid = (pl.cdiv(n, tn), rows // tm)
    row_spec = pl.BlockSpec((tm, d), lambda j, i: (i, 0))
    w_spec = pl.BlockSpec((None, d, tn), lambda j, i: (layer, 0, j))
    cw_spec = pl.BlockSpec((CONV_W, tn), lambda j, i: (0, j))
    cb_spec = pl.BlockSpec((1, tn), lambda j, i: (0, j))
    out_spec = pl.BlockSpec((tm, tn), lambda j, i: (i, j))
    out_shape = (jax.ShapeDtypeStruct((rows, n), BF16), jax.ShapeDtypeStruct((rows // seq_len, 8, n), F32))
    cb = conv_b.reshape(1, n)
    if hist is None:
        assert seq_len % tm == 0
        return pl.pallas_call(
            functools.partial(_gateup_carry_kernel, tiles_per_seq=seq_len // tm),
            out_shape=out_shape, grid=grid,
            in_specs=[row_spec, w_spec, w_spec, cw_spec, cb_spec],
            out_specs=(out_spec, pl.BlockSpec((1, 8, tn), lambda j, i: (i // (seq_len // tm), 0, j))),
            scratch_shapes=[pltpu.VMEM((tm + 8, tn), F32)],
            compiler_params=_params("arbitrary", "arbitrary"),
            name="gateup_carry",
        )(h, w_gate, w_up, conv_w, cb)
    assert tm % seq_len == 0
    nb = rows // seq_len
    zeros = jnp.zeros((nb, seq_len, n), F32)
    h1 = zeros.at[:, 0].set(hist[:, 1]).reshape(rows, n)
    h2 = zeros.at[:, 0].set(hist[:, 0]).at[:, 1].set(hist[:, 1]).reshape(rows, n)
    hist_spec = pl.BlockSpec((tm, tn), lambda j, i: (i, j))
    return pl.pallas_call(
        functools.partial(_gateup_hist_kernel, seq_len=seq_len),
        out_shape=out_shape, grid=grid,
        in_specs=[row_spec, w_spec, w_spec, cw_spec, cb_spec, hist_spec, hist_spec],
        out_specs=(out_spec, pl.BlockSpec((tm // seq_len, 8, tn), lambda j, i: (i, 0, j))),
        compiler_params=_params("parallel", "parallel"),
        name="gateup_hist",
    )(h, w_gate, w_up, conv_w, cb, h1, h2)


def _down_kernel(a_ref, w_ref, x_ref, o_ref):
    o_ref[...] = x_ref[...] + jnp.dot(a_ref[...], w_ref[...], preferred_element_type=F32)


def _down(act, w_down, layer, x, tm):
    rows, n = act.shape
    d = x.shape[1]
    tn = min(COL_TILE // 2, d)
    return pl.pallas_call(
        _down_kernel,
        out_shape=jax.ShapeDtypeStruct((rows, d), F32),
        grid=(rows // tm, d // tn),
        in_specs=[pl.BlockSpec((tm, n), lambda i, j: (i, 0)),
                  pl.BlockSpec((None, n, tn), lambda i, j: (layer, 0, j)),
                  pl.BlockSpec((tm, tn), lambda i, j: (i, j))],
        out_specs=pl.BlockSpec((tm, tn), lambda i, j: (i, j)),
        compiler_params=_params("parallel", "arbitrary"),
        name="down",
    )(act, w_down, x)


def _sb_logs(z, mask):
    neg_abs = lax.bitcast_convert_type(lax.bitcast_convert_type(z, jnp.int32) | SIGN_BIT, F32)
    sp = jnp.maximum(z, 0.0) + jnp.log2(1.0 + jnp.exp2(neg_abs))
    return z - sp, (sp if mask is None else jnp.where(mask, sp, 0.0))


def _sb_split(drop):
    hi = drop.astype(BF16)
    return hi, (drop - hi.astype(F32)).astype(BF16)


def _sb_prompt_kernel(aq_ref, aj_ref, bq_ref, dq_ref, dj_ref, q_ref, k_ref, v_ref, u_ref, o_ref,
                      qb, kb, vb, acc, car, z_s, beta_s, hi_s, lo_s, w_s,
                      *, scale, t_len, n_steps):
    blk = SB_BLOCK
    t_pad = kb.shape[0]
    nq = t_pad // blk
    nt = (((1,), (1,)), ((), ()))

    for src, dst, mul in ((q_ref, qb, scale), (k_ref, kb, None), (v_ref, vb, None)):
        x = src[0] if mul is None else src[0] * mul
        dst[0:t_len, :] = x.astype(BF16)
        if t_pad > t_len:
            dst[t_len:t_pad, :] = jnp.zeros((t_pad - t_len, HEAD_DIM), BF16)
    for ref in (z_s, beta_s, hi_s, lo_s, w_s):
        ref[...] = jnp.zeros_like(ref)

    row = lax.broadcasted_iota(jnp.int32, (blk, blk), 0)
    col = lax.broadcasted_iota(jnp.int32, (blk, blk), 1)
    causal = col < row

    def rows(i):
        return pl.ds(pl.multiple_of(i * blk, blk), blk)

    def cumsum(hi, lo):
        return (jnp.dot(hi, u_ref[...], preferred_element_type=F32)
                + jnp.dot(lo, u_ref[...], preferred_element_type=F32))

    def diag(blocks):
        rs = [rows(qi) for qi in blocks]
        logs = [_sb_logs(lax.dot_general(qb[r, :], kb[r, :], nt, preferred_element_type=F32), causal) for r in rs]
        gaps = [cumsum(*_sb_split(drop)) for _, drop in logs]
        for qi, r, (beta, drop), gap in zip(blocks, rs, logs, gaps):
            w = jnp.where(causal, jnp.exp2(beta - gap), 0.0)
            acc[qi] = jnp.dot(w.astype(BF16), vb[r, :], preferred_element_type=F32)
            car[qi] = jnp.sum(drop, axis=-1, keepdims=True)

    def diag_pair(i, c):
        diag([2 * i, 2 * i + 1])
        return c

    lax.fori_loop(0, nq // 2, diag_pair, 0)
    if nq % 2:
        diag([nq - 1])

    def step(t, slot):
        other = 1 - slot
        acc[dq_ref[t]] += jnp.dot(w_s[other], vb[rows(dj_ref[t]), :], preferred_element_type=F32)
        w_s[slot] = jnp.exp2(beta_s[slot] - cumsum(hi_s[slot], lo_s[slot])).astype(BF16)
        z_s[slot] = lax.dot_general(qb[rows(aq_ref[t]), :], kb[rows(aj_ref[t]), :], nt,
                                    preferred_element_type=F32)
        q1 = bq_ref[t]
        beta, drop = _sb_logs(z_s[other], None)
        carry = car[q1]
        beta_s[other] = beta - carry
        hi_s[other], lo_s[other] = _sb_split(drop)
        car[q1] = carry + jnp.sum(drop, axis=-1, keepdims=True)

    def steps(it, c):
        for s in range(SB_UNROLL):
            step(it * SB_UNROLL + s, s % 2)
        return c

    acc[nq] = jnp.zeros((blk, HEAD_DIM), F32)
    car[nq] = jnp.zeros((blk, 1), F32)
    lax.fori_loop(0, n_steps // SB_UNROLL, steps, 0)
    for qi in range(nq):
        r = slice(qi * blk, min((qi + 1) * blk, t_len))
        o_ref[0, r, :] = acc[qi, 0:r.stop - r.start, :].astype(o_ref.dtype)


def _tri_ones(n):
    r = lax.broadcasted_iota(jnp.int32, (n, n), 0)
    c = lax.broadcasted_iota(jnp.int32, (n, n), 1)
    return (r > c).astype(BF16)


def _sb_prompt(proj, n_heads):
    b, t, _ = proj.shape
    blk = SB_BLOCK
    nq = pl.cdiv(t, blk)
    t_pad = nq * blk
    scale = LOG2_E / math.sqrt(HEAD_DIM)
    items = [(qi, j) for qi in range(1, nq) for j in range(qi - 1, -1, -1)]
    n_steps = 0 if not items else -(-(len(items) + 3) // SB_UNROLL) * SB_UNROLL
    item = lambda m: items[m] if 0 <= m < len(items) else None
    table = lambda delay, pick, spare: jnp.asarray(
        [spare if item(s - delay) is None else item(s - delay)[pick] for s in range(n_steps)] + [spare], jnp.int32)
    tables = (table(0, 0, 0), table(0, 1, 0), table(1, 0, nq), table(3, 0, nq), table(3, 1, 0))
    head = lambda off: pl.BlockSpec((1, t, HEAD_DIM), lambda bi, h, *_: (bi, 0, off + h))
    ring = lambda dt: pltpu.VMEM((2, blk, blk), dt)
    return pl.pallas_call(
        functools.partial(_sb_prompt_kernel, scale=scale, t_len=t, n_steps=n_steps),
        out_shape=jax.ShapeDtypeStruct((b, t, n_heads * HEAD_DIM), BF16),
        grid_spec=pltpu.PrefetchScalarGridSpec(
            num_scalar_prefetch=5,
            grid=(b, n_heads),
            in_specs=[head(0), head(n_heads), head(2 * n_heads),
                      pl.BlockSpec((blk, blk), lambda bi, h, *_: (0, 0))],
            out_specs=pl.BlockSpec((1, t, HEAD_DIM), lambda bi, h, *_: (bi, 0, h)),
            scratch_shapes=[pltpu.VMEM((t_pad, HEAD_DIM), BF16)] * 3
                           + [pltpu.VMEM((nq + 1, blk, HEAD_DIM), F32), pltpu.VMEM((nq + 1, blk, 1), F32),
                              ring(F32), ring(F32), ring(BF16), ring(BF16), ring(BF16)]),
        compiler_params=_params("parallel", "parallel"),
        name="sb_prompt",
    )(*tables, proj, proj, proj, _tri_ones(blk))


def _sb_sample_kernel(q_ref, kn_ref, vn_ref, kc_ref, vc_ref, u_ref, o_ref, *, scale, n_cache_blocks):
    blk = SB_BLOCK
    n = q_ref.shape[1]
    nt = (((1,), (1,)), ((), ()))
    q = (q_ref[0] * scale).astype(BF16)
    row = lax.broadcasted_iota(jnp.int32, (n, n), 0)
    col = lax.broadcasted_iota(jnp.int32, (n, n), 1)
    keys = [kn_ref[0].astype(BF16)] + [kc_ref[0, j * blk:(j + 1) * blk, :].astype(BF16)
                                       for j in reversed(range(n_cache_blocks))]
    vals = [vn_ref[0].astype(BF16)] + [vc_ref[0, j * blk:(j + 1) * blk, :].astype(BF16)
                                       for j in reversed(range(n_cache_blocks))]
    masks = [col < row] + [None] * n_cache_blocks
    tris = [u_ref[0:n, 0:n]] + [u_ref[...]] * n_cache_blocks
    logs = [_sb_logs(lax.dot_general(q, k, nt, preferred_element_type=F32), m) for k, m in zip(keys, masks)]
    gaps = []
    for (_, drop), u in zip(logs, tris):
        hi, lo = _sb_split(drop)
        gaps.append(jnp.dot(hi, u, preferred_element_type=F32) + jnp.dot(lo, u, preferred_element_type=F32))
    carry = jnp.zeros((n, 1), F32)
    acc = jnp.zeros((n, HEAD_DIM), F32)
    for (beta, drop), gap, v, m in zip(logs, gaps, vals, masks):
        w = jnp.exp2(beta - gap - carry)
        if m is not None:
            w = jnp.where(m, w, 0.0)
        acc = acc + jnp.dot(w.astype(BF16), v, preferred_element_type=F32)
        carry = carry + jnp.sum(drop, axis=-1, keepdims=True)
    o_ref[0] = acc.astype(o_ref.dtype)


def _sb_sample(proj, cache_k, cache_v, layer, n_heads):
    b, n, _ = proj.shape
    p = cache_k.shape[2]
    blk = SB_BLOCK
    assert p % blk == 0 and n <= blk
    scale = LOG2_E / math.sqrt(HEAD_DIM)
    new_spec = lambda off: pl.BlockSpec((1, n, HEAD_DIM), lambda bi, h: (bi, 0, off + h))
    cache_spec = pl.BlockSpec((None, 1, p, HEAD_DIM), lambda bi, h: (layer, bi, 0, h))
    return pl.pallas_call(
        functools.partial(_sb_sample_kernel, scale=scale, n_cache_blocks=p // blk),
        out_shape=jax.ShapeDtypeStruct((b, n, n_heads * HEAD_DIM), BF16),
        grid=(b, n_heads),
        in_specs=[new_spec(0), new_spec(n_heads), new_spec(2 * n_heads), cache_spec, cache_spec,
                  pl.BlockSpec((blk, blk), lambda bi, h: (0, 0))],
        out_specs=pl.BlockSpec((1, n, HEAD_DIM), lambda bi, h: (bi, 0, h)),
        compiler_params=_params("parallel", "parallel"),
        name="sb_sample",
    )(proj, proj, proj, cache_k, cache_v, _tri_ones(blk))


def _pool_kernel(x_ref, hist_ref, w_ref, scale_ref, o_ref, ext_ref, *, n_hist, group_dim):
    ti = pl.program_id(1)
    tm = x_ref.shape[1]
    pad = POOL_HIST + 1

    @pl.when(ti == 0)
    def _():
        ext_ref[0:pad, :] = hist_ref[0]

    @pl.when(ti != 0)
    def _():
        ext_ref[0:pad, :] = ext_ref[tm:tm + pad, :]

    ext_ref[pad:pad + tm, :] = x_ref[0]
    t = ti * tm + lax.broadcasted_iota(jnp.int32, (tm, 1), 0)
    for gi, win in enumerate(POOL_WINDOWS):
        cols = slice(gi * group_dim, (gi + 1) * group_dim)
        x = ext_ref[pad:pad + tm, cols]
        total = x
        for d in range(1, win):
            total = total + ext_ref[pad - d:pad - d + tm, cols]
        count = jnp.minimum(t + 1 + n_hist, win).astype(F32)
        pooled = total / count - x
        y = jnp.dot(pooled.astype(BF16), w_ref[gi], preferred_element_type=F32)
        o_ref[0, :, cols] = (y * scale_ref[:, cols]).astype(o_ref.dtype)


def _pool(proj, col_block, width, hist, n_hist, w_pool, scale, tm):
    b, t, _ = proj.shape
    groups = len(POOL_WINDOWS)
    group_dim = width // groups
    pad = POOL_HIST + 1
    hist_pad = jnp.concatenate([jnp.zeros((b, 1, width), F32), hist], axis=1)
    return pl.pallas_call(
        functools.partial(_pool_kernel, n_hist=n_hist, group_dim=group_dim),
        out_shape=jax.ShapeDtypeStruct((b, t, width), BF16),
        grid=(b, t // tm),
        in_specs=[pl.BlockSpec((1, tm, width), lambda bi, ti: (bi, ti, col_block)),
                  pl.BlockSpec((1, pad, width), lambda bi, ti: (bi, 0, 0)),
                  pl.BlockSpec((groups, group_dim, group_dim), lambda bi, ti: (0, 0, 0)),
                  pl.BlockSpec((1, width), lambda bi, ti: (0, 0))],
        out_specs=pl.BlockSpec((1, tm, width), lambda bi, ti: (bi, ti, 0)),
        scratch_shapes=[pltpu.VMEM((tm + pad, width), F32)],
        compiler_params=_params("parallel", "arbitrary"),
        name="pool",
    )(proj, hist_pad, w_pool, scale.reshape(1, width))


def _cumsum_rows(x):
    row = lax.broadcasted_iota(jnp.int32, x.shape, 0)
    shift = 1
    while shift < x.shape[0]:
        x = x + jnp.where(row >= shift, pltpu.roll(x, shift, 0), 0.0)
        shift *= 2
    return x


def _hgrn_kernel(q_ref, f_ref, i_ref, g_ref, la_ref, lc_ref, ng_ref, ones_ref, s0_ref, o_ref, sout_ref,
                 s_ref, ck_ref, vs_ref, *, n_heads, n_chunks):
    tb = pl.program_id(1)
    half = HG_CHUNK // 2
    hd = HEAD_DIM

    @pl.when(tb == 0)
    def _():
        s_ref[...] = s0_ref[0]

    la = la_ref[...]
    lc = lc_ref[...]
    ng = ng_ref[...]
    row8 = lax.broadcasted_iota(jnp.int32, (half, hd), 0)

    def chunk(c, carry):
        r0 = pl.multiple_of(c * HG_CHUNK, HG_CHUNK)
        qc = q_ref[0, pl.ds(r0, HG_CHUNK), :]
        fc = f_ref[0, pl.ds(r0, HG_CHUNK), :]
        gc = g_ref[0, pl.ds(r0, HG_CHUNK), :]
        q = qc * _sigmoid(qc)
        gate = gc * _sigmoid(gc)
        l1 = jnp.log(1.0 + jnp.exp(-jnp.abs(fc)))
        log_sig = jnp.minimum(fc, 0.0) - l1
        log_sig_neg = jnp.minimum(-fc, 0.0) - l1
        b = lc + log_sig
        log_f = jnp.maximum(la, b) + jnp.log(1.0 + jnp.exp(-jnp.abs(la - b)))
        log2_k = (lc + log_sig_neg) * LOG2_E
        cum = _cumsum_rows(log_f * LOG2_E)
        last = cum[HG_CHUNK - 1:HG_CHUNK, :]
        qt = (q * jnp.exp2(cum)).astype(BF16)
        kt = jnp.exp2(log2_k + last - cum).astype(BF16)
        dec = jnp.exp2(last)
        ck_ref[...] = cum - log2_k
        v = i_ref[0, pl.ds(r0, HG_CHUNK), :]
        vs_ref[...] = v
        vb = v.astype(BF16)
        for h in range(n_heads):
            sl = slice(h * hd, (h + 1) * hd)
            cum_a, cum_b = cum[0:half, sl], cum[half:, sl]
            q_a, q_b = q[0:half, sl], q[half:, sl]
            pairs = []
            for s in range(HG_CHUNK):
                cs = jnp.broadcast_to(ck_ref[s:s + 1, sl], (half, hd))
                if s < half:
                    pairs.append(q_a * jnp.where(row8 >= s, jnp.exp2(cum_a - cs), 0.0))
                    pairs.append(q_b * jnp.exp2(cum_b - cs))
                else:
                    pairs.append(q_b * jnp.where(row8 + half >= s, jnp.exp2(cum_b - cs), 0.0))
            scores = jnp.dot(jnp.concatenate(pairs, axis=0).astype(BF16), ones_ref[...],
                             preferred_element_type=F32)
            o_a = jnp.zeros((half, hd), F32)
            o_b = jnp.zeros((half, hd), F32)
            blocks = iter(range(len(pairs)))
            for s in range(HG_CHUNK):
                vs = jnp.broadcast_to(vs_ref[s:s + 1, sl], (half, hd))
                if s < half:
                    i = next(blocks)
                    o_a = o_a + scores[i * half:(i + 1) * half] * vs
                i = next(blocks)
                o_b = o_b + scores[i * half:(i + 1) * half] * vs
            st = s_ref[h]
            inter = lax.dot_general(qt[:, sl], st.astype(BF16), (((1,), (1,)), ((), ())),
                                    preferred_element_type=F32)
            o = jnp.concatenate([o_a, o_b], axis=0) + inter
            ms = jnp.mean(o * o, axis=-1, keepdims=True)
            o = o * lax.rsqrt(ms + EPS) * ng * gate[:, sl]
            o_ref[0, pl.ds(r0, HG_CHUNK), sl] = o.astype(o_ref.dtype)
            s_ref[h] = st * dec[:, sl] + lax.dot_general(vb[:, sl], kt[:, sl], (((0,), (0,)), ((), ())),
                                                         preferred_element_type=F32)
        return carry

    lax.fori_loop(0, n_chunks, chunk, 0, unroll=max(u for u in (1, 2, 3) if n_chunks % u == 0))

    @pl.when(tb == pl.num_programs(1) - 1)
    def _():
        sout_ref[0] = s_ref[...]


def _hgrn(proj, first_block, width, log_lb, log_1m_lb, norm_g, s0_t, tb_rows):
    b, t, _ = proj.shape
    n_heads = width // HEAD_DIM
    assert t % tb_rows == 0 and tb_rows % HG_CHUNK == 0
    col = lambda off: pl.BlockSpec((1, tb_rows, width), lambda bi, ti: (bi, ti, first_block + off))
    vec = pl.BlockSpec((1, width), lambda bi, ti: (0, 0))
    state_spec = pl.BlockSpec((1, n_heads, HEAD_DIM, HEAD_DIM), lambda bi, ti: (bi, 0, 0, 0))
    return pl.pallas_call(
        functools.partial(_hgrn_kernel, n_heads=n_heads, n_chunks=tb_rows // HG_CHUNK),
        out_shape=(jax.ShapeDtypeStruct((b, t, width), BF16),
                   jax.ShapeDtypeStruct((b, n_heads, HEAD_DIM, HEAD_DIM), F32)),
        grid=(b, t // tb_rows),
        in_specs=[col(0), col(1), col(2), col(3), vec, vec,
                  pl.BlockSpec((1, HEAD_DIM), lambda bi, ti: (0, 0)),
                  pl.BlockSpec((HEAD_DIM, HEAD_DIM), lambda bi, ti: (0, 0)), state_spec],
        out_specs=(pl.BlockSpec((1, tb_rows, width), lambda bi, ti: (bi, ti, 0)), state_spec),
        scratch_shapes=[pltpu.VMEM((n_heads, HEAD_DIM, HEAD_DIM), F32), pltpu.VMEM((HG_CHUNK, width), F32),
                        pltpu.VMEM((HG_CHUNK, width), F32)],
        compiler_params=_params("parallel", "arbitrary"),
        name="hgrn",
    )(proj, proj, proj, proj, log_lb.reshape(1, width), log_1m_lb.reshape(1, width),
      norm_g.reshape(1, HEAD_DIM).astype(F32), jnp.ones((HEAD_DIM, HEAD_DIM), BF16), s0_t)


def _layer(x, batch, seq, lw, layer, sb_cache, pool_hist, n_hist, s0, conv_hist, kv_stack=None):
    rows, d = x.shape
    sb_width = d // 4
    pool_width = d // 4
    hg_width = d // 2
    n_sb_heads = sb_width // HEAD_DIM
    tm = _row_tile(seq, MAX_ROW_TILE) if conv_hist is None else rows
    assert rows % tm == 0

    h = _rmsnorm(x, lw["norm1_g"], BF16)
    proj = _matmul(h, lw["w_in"], layer, tm)
    proj3 = proj.reshape(batch, seq, 3 * d)
    if sb_cache is None:
        a_out = _sb_prompt(proj3, n_sb_heads)
    else:
        a_out = _sb_sample(proj3, sb_cache[0], sb_cache[1], layer, n_sb_heads)
    seq_tile = _row_tile(seq, MAX_ROW_TILE)
    b_out = _pool(proj3, 3, pool_width, pool_hist, n_hist, lw["pool_w"], lw["pool_scale"], seq_tile)
    c_out, s_new_t = _hgrn(proj3, 2, hg_width, lw["log_lb"], lw["log_1m_lb"], lw["hgrn_norm_g"],
                           jnp.swapaxes(s0, -1, -2), _row_tile(seq, MAX_HGRN_ROWS))
    x = _outproj(a_out.reshape(rows, sb_width), b_out.reshape(rows, pool_width),
                 c_out.reshape(rows, hg_width), lw["w_out"], layer, x, tm)

    h2 = _rmsnorm(x, lw["norm2_g"], BF16)
    act, g_tail = _gateup(h2, lw["w_gate"], lw["w_up"], layer, lw["conv_w"], lw["conv_b"], tm, seq, conv_hist)
    x = _down(act, lw["w_down"], layer, x, tm)
    conv_state = g_tail[:, 8 - (CONV_W - 1):, :]

    if kv_stack is None:
        k_new = proj3[:, :, sb_width:2 * sb_width].reshape(batch, seq, n_sb_heads, HEAD_DIM)
        v_new = proj3[:, :, 2 * sb_width:3 * sb_width].reshape(batch, seq, n_sb_heads, HEAD_DIM)
    else:
        n_layers, k_stack, v_stack = kv_stack
        k_new = _to_heads(proj3, 1, n_sb_heads, k_stack, n_layers, layer, seq_tile)
        v_new = _to_heads(proj3, 2, n_sb_heads, v_stack, n_layers, layer, seq_tile)
    xb = proj3[:, :, 3 * sb_width:3 * sb_width + pool_width]
    pool_new = jnp.concatenate([pool_hist, xb], axis=1)[:, -POOL_HIST:]
    s_new = jnp.swapaxes(s_new_t, -1, -2)
    return x, (k_new, v_new, s_new, pool_new, conv_state)


def kernel(x_prompt, x_sample, cache_sb_k, cache_sb_v, state_hgrn, state_pool, state_conv, meta_tokens,
           norm1_g, w_in, pool_w, pool_scale, hgrn_lower_bounds, hgrn_norm_g, w_out, norm2_g, ffn_w_gate,
           ffn_w_up, ffn_conv_w, ffn_conv_b, ffn_w_down, final_norm_g):
    bp, seq_p, d = x_prompt.shape
    bs, seq_s, _ = x_sample.shape
    depth = w_in.shape[0]
    tp = N_META + seq_p
    hg_heads = (d // 2) // HEAD_DIM

    meta = jnp.broadcast_to(meta_tokens[None], (bp, N_META, d))
    xp = jnp.concatenate([meta, x_prompt], axis=1).reshape(bp * tp, d)
    xs = x_sample.reshape(bs * seq_s, d)

    probs = jax.nn.softmax(hgrn_lower_bounds.astype(F32), axis=0)
    lower = jnp.maximum(jnp.cumsum(probs, axis=0) - probs[0], 0.0)
    log_lb = jnp.log(lower)
    log_1m_lb = jnp.log1p(-lower)

    w_in_b, w_out_b = w_in.astype(BF16), w_out.astype(BF16)
    w_gate_b, w_up_b, w_down_b = ffn_w_gate.astype(BF16), ffn_w_up.astype(BF16), ffn_w_down.astype(BF16)
    pool_w_b = pool_w.astype(BF16)
    cache = (cache_sb_k.reshape(depth, bs, -1, d // 4), cache_sb_v.reshape(depth, bs, -1, d // 4))

    outs_p, outs_s = [], []
    pk = pv = None
    for l in range(depth):
        lw = dict(norm1_g=norm1_g[l], w_in=w_in_b, pool_w=pool_w_b[l], pool_scale=pool_scale[l],
                  log_lb=log_lb[l], log_1m_lb=log_1m_lb[l], hgrn_norm_g=hgrn_norm_g[l], w_out=w_out_b,
                  norm2_g=norm2_g[l], w_gate=w_gate_b, w_up=w_up_b, conv_w=ffn_conv_w[l], conv_b=ffn_conv_b[l],
                  w_down=w_down_b)
        xp, out = _layer(xp, bp, tp, lw, l, None, jnp.zeros((bp, POOL_HIST, d // 4), F32), 0,
                         jnp.zeros((bp, hg_heads, HEAD_DIM, HEAD_DIM), F32), None, (depth, pk, pv))
        pk, pv = out[0], out[1]
        outs_p.append(out)
        xs, out = _layer(xs, bs, seq_s, lw, l, cache, state_pool[l], POOL_HIST, state_hgrn[l], state_conv[l])
        outs_s.append(out)

    y_prompt = _rmsnorm_skip(xp.reshape(bp, tp, d), final_norm_g, N_META, F32)
    y_sample = _rmsnorm(xs, final_norm_g, F32).reshape(bs, seq_s, d)
    stack = lambda outs, idx: jnp.stack([o[idx] for o in outs])
    return (y_prompt, y_sample,
            pk, pv, stack(outs_p, 2), stack(outs_p, 3), stack(outs_p, 4),
            stack(outs_s, 0), stack(outs_s, 1), stack(outs_s, 2), stack(outs_s, 3), stack(outs_s, 4))
```

```python
import functools
import math

import jax
import jax.numpy as jnp
from jax import lax
from jax.experimental import pallas as pl
from jax.experimental.pallas import tpu as pltpu

F32 = jnp.float32
BF16 = jnp.bfloat16

N_META = 16
EPS = 1e-6
HEAD_DIM = 128
POOL_WINDOWS = (2, 4, 8, 16)
POOL_HIST = max(POOL_WINDOWS) - 1
CONV_W = 3

VMEM_LIMIT_BYTES = 56 * 1024 * 1024
BF16_SUBLANES = 16
MAX_ROW_TILE = 1024
MAX_NORM_ROWS = 512
MAX_HGRN_ROWS = 256
SB_BLOCK = 256
SB_SAMPLE_HEADS = 2
SB_UNROLL = 4
HG_CHUNK = 16
COL_TILE = 512
LANE = 128
SIGN_BIT = -2 ** 31
LOG2_E = 1.4426950408889634


def _row_tile(rows, limit):
    best = None
    for t in range(BF16_SUBLANES, min(rows, limit) + 1, BF16_SUBLANES):
        if rows % t == 0:
            best = t
    assert best is not None, rows
    return best


def _params(*semantics):
    return pltpu.CompilerParams(dimension_semantics=semantics, vmem_limit_bytes=VMEM_LIMIT_BYTES)


def _sigmoid(x):
    return 1.0 / (1.0 + jnp.exp(-x))


def _rmsnorm_kernel(x_ref, g_ref, o_ref):
    x = x_ref[...]
    ms = jnp.mean(x * x, axis=-1, keepdims=True)
    o_ref[...] = (x * lax.rsqrt(ms + EPS) * g_ref[...]).astype(o_ref.dtype)


def _rmsnorm_skip(x, g, skip, out_dtype):
    b, t, d = x.shape
    tr = _row_tile(t - skip, MAX_NORM_ROWS)
    assert skip % 8 == 0
    return pl.pallas_call(
        _rmsnorm_kernel,
        out_shape=jax.ShapeDtypeStruct((b, t - skip, d), out_dtype),
        grid=(b, (t - skip) // tr),
        in_specs=[pl.BlockSpec((pl.Element(1), pl.Element(tr), pl.Element(d)),
                               lambda bi, i: (bi, pl.multiple_of(skip + i * tr, 8), 0)),
                  pl.BlockSpec((1, 1, d), lambda bi, i: (0, 0, 0))],
        out_specs=pl.BlockSpec((1, tr, d), lambda bi, i: (bi, i, 0)),
        compiler_params=_params("parallel", "parallel"),
        name="rmsnorm_skip",
    )(x, g.reshape(1, 1, d).astype(F32))


def _rmsnorm(x, g, out_dtype):
    rows, d = x.shape
    tr = _row_tile(rows, MAX_NORM_ROWS)
    return pl.pallas_call(
        _rmsnorm_kernel,
        out_shape=jax.ShapeDtypeStruct((rows, d), out_dtype),
        grid=(rows // tr,),
        in_specs=[pl.BlockSpec((tr, d), lambda i: (i, 0)),
                  pl.BlockSpec((1, d), lambda i: (0, 0))],
        out_specs=pl.BlockSpec((tr, d), lambda i: (i, 0)),
        compiler_params=_params("parallel"),
        name="rmsnorm",
    )(x, g.reshape(1, d).astype(F32))


def _matmul_kernel(x_ref, w_ref, o_ref):
    o_ref[...] = jnp.dot(x_ref[...], w_ref[...], preferred_element_type=F32)


def _matmul(x, w, layer, tm):
    rows, k = x.shape
    n = w.shape[2]
    tn = min(2 * COL_TILE, n)
    return pl.pallas_call(
        _matmul_kernel,
        out_shape=jax.ShapeDtypeStruct((rows, n), F32),
        grid=(pl.cdiv(n, tn), rows // tm),
        in_specs=[pl.BlockSpec((tm, k), lambda j, i: (i, 0)),
                  pl.BlockSpec((None, k, tn), lambda j, i: (layer, 0, j))],
        out_specs=pl.BlockSpec((tm, tn), lambda j, i: (i, j)),
        compiler_params=_params("parallel", "parallel"),
        name="matmul",
    )(x, w)


def _heads_kernel(x_ref, *refs, n_heads):
    o_ref = refs[-1]
    for h in range(n_heads):
        o_ref[0, 0, :, h, :] = x_ref[0, :, h * HEAD_DIM:(h + 1) * HEAD_DIM]


def _to_heads(proj, col_block, n_heads, stacked, n_layers, layer, tm):
    b, t, _ = proj.shape
    shape = (n_layers, b, t, n_heads, HEAD_DIM)
    prev = () if stacked is None else (stacked,)
    return pl.pallas_call(
        functools.partial(_heads_kernel, n_heads=n_heads),
        out_shape=jax.ShapeDtypeStruct(shape, F32),
        grid=(b, t // tm),
        in_specs=[pl.BlockSpec((1, tm, n_heads * HEAD_DIM), lambda bi, i: (bi, i, col_block))]
                 + [pl.BlockSpec(memory_space=pl.ANY)] * len(prev),
        out_specs=pl.BlockSpec((1, 1, tm, n_heads, HEAD_DIM), lambda bi, i: (layer, bi, i, 0, 0)),
        input_output_aliases={1: 0} if prev else {},
        compiler_params=_params("parallel", "parallel"),
        name="to_heads",
    )(proj, *prev)


def _outproj_kernel(a_ref, b_ref, c_ref, wa_ref, wb_ref, wc_ref, x_ref, o_ref):
    acc = jnp.dot(a_ref[...], wa_ref[...], preferred_element_type=F32)
    acc += jnp.dot(b_ref[...], wb_ref[...], preferred_element_type=F32)
    acc += jnp.dot(c_ref[...], wc_ref[...], preferred_element_type=F32)
    o_ref[...] = x_ref[...] + acc


def _outproj(a, b, c, w_out, layer, x, tm):
    rows, d = x.shape
    wa, wb, wc = a.shape[1], b.shape[1], c.shape[1]
    assert wa == wb and wc == 2 * wa and w_out.shape[1] == wa + wb + wc
    tn = min(2 * COL_TILE, d)
    return pl.pallas_call(
        _outproj_kernel,
        out_shape=jax.ShapeDtypeStruct((rows, d), F32),
        grid=(d // tn, rows // tm),
        in_specs=[pl.BlockSpec((tm, wa), lambda j, i: (i, 0)),
                  pl.BlockSpec((tm, wb), lambda j, i: (i, 0)),
                  pl.BlockSpec((tm, wc), lambda j, i: (i, 0)),
                  pl.BlockSpec((None, wa, tn), lambda j, i: (layer, 0, j)),
                  pl.BlockSpec((None, wb, tn), lambda j, i: (layer, 1, j)),
                  pl.BlockSpec((None, wc, tn), lambda j, i: (layer, 1, j)),
                  pl.BlockSpec((tm, tn), lambda j, i: (i, j))],
        out_specs=pl.BlockSpec((tm, tn), lambda j, i: (i, j)),
        compiler_params=_params("parallel", "parallel"),
        name="outproj",
    )(a, b, c, w_out, w_out, w_out, x)


def _ffn_act(g, g1, g2, u, cw_ref, cb_ref):
    conv = cb_ref[...] + g2 * cw_ref[0:1, :] + g1 * cw_ref[1:2, :] + g * cw_ref[2:3, :]
    return (conv * _sigmoid(conv) * u).astype(BF16)


def _gateup_carry_kernel(h_ref, wg_ref, wu_ref, cw_ref, cb_ref, o_ref, tail_ref, g_ref, *, tiles_per_seq):
    i = pl.program_id(1)
    tm = h_ref.shape[0]

    @pl.when(i % tiles_per_seq == 0)
    def _():
        g_ref[0:8, :] = jnp.zeros((8, g_ref.shape[1]), F32)

    @pl.when(i % tiles_per_seq != 0)
    def _():
        g_ref[0:8, :] = g_ref[tm:tm + 8, :]

    h = h_ref[...]
    g_ref[8:tm + 8, :] = jnp.dot(h, wg_ref[...], preferred_element_type=F32)
    u = jnp.dot(h, wu_ref[...], preferred_element_type=F32)
    o_ref[...] = _ffn_act(g_ref[8:tm + 8, :], g_ref[7:tm + 7, :], g_ref[6:tm + 6, :], u, cw_ref, cb_ref)
    tail_ref[0] = g_ref[tm:tm + 8, :]


def _gateup_hist_kernel(h_ref, wg_ref, wu_ref, cw_ref, cb_ref, h1_ref, h2_ref, o_ref, tail_ref, *, seq_len):
    h = h_ref[...]
    g = jnp.dot(h, wg_ref[...], preferred_element_type=F32)
    u = jnp.dot(h, wu_ref[...], preferred_element_type=F32)
    pos = lax.broadcasted_iota(jnp.int32, g.shape, 0) % seq_len
    g1 = jnp.where(pos == 0, h1_ref[...], pltpu.roll(g, 1, 0))
    g2 = jnp.where(pos < 2, h2_ref[...], pltpu.roll(g, 2, 0))
    o_ref[...] = _ffn_act(g, g1, g2, u, cw_ref, cb_ref)
    tail_ref[...] = g.reshape(g.shape[0] // seq_len, seq_len, g.shape[1])[:, seq_len - 8:, :]


def _gateup(h, w_gate, w_up, layer, conv_w, conv_b, tm, seq_len, hist):
    rows, d = h.shape
    n = w_gate.shape[2]
    tn = min(COL_TILE, n)
    grid = (pl.cdiv(n, tn), rows // tm)
    row_spec = pl.BlockSpec((tm, d), lambda j, i: (i, 0))
    w_spec = pl.BlockSpec((None, d, tn), lambda j, i: (layer, 0, j))
    cw_spec = pl.BlockSpec((CONV_W, tn), lambda j, i: (0, j))
    cb_spec = pl.BlockSpec((1, tn), lambda j, i: (0, j))
    out_spec = pl.BlockSpec((tm, tn), lambda j, i: (i, j))
    out_shape = (jax.ShapeDtypeStruct((rows, n), BF16), jax.ShapeDtypeStruct((rows // seq_len, 8, n), F32))
    cb = conv_b.reshape(1, n)
    if hist is None:
        assert seq_len % tm == 0
        return pl.pallas_call(
            functools.partial(_gateup_carry_kernel, tiles_per_seq=seq_len // tm),
            out_shape=out_shape, grid=grid,
            in_specs=[row_spec, w_spec, w_spec, cw_spec, cb_spec],
            out_specs=(out_spec, pl.BlockSpec((1, 8, tn), lambda j, i: (i // (seq_len // tm), 0, j))),
            scratch_shapes=[pltpu.VMEM((tm + 8, tn), F32)],
            compiler_params=_params("arbitrary", "arbitrary"),
            name="gateup_carry",
        )(h, w_gate, w_up, conv_w, cb)
    assert tm % seq_len == 0
    nb = rows // seq_len
    zeros = jnp.zeros((nb, seq_len, n), F32)
    h1 = zeros.at[:, 0].set(hist[:, 1]).reshape(rows, n)
    h2 = zeros.at[:, 0].set(hist[:, 0]).at[:, 1].set(hist[:, 1]).reshape(rows, n)
    hist_spec = pl.BlockSpec((tm, tn), lambda j, i: (i, j))
    return pl.pallas_call(
        functools.partial(_gateup_hist_kernel, seq_len=seq_len),
        out_shape=out_shape, grid=grid,
        in_specs=[row_spec, w_spec, w_spec, cw_spec, cb_spec, hist_spec, hist_spec],
        out_specs=(out_spec, pl.BlockSpec((tm // seq_len, 8, tn), lambda j, i: (i, 0, j))),
        compiler_params=_params("parallel", "parallel"),
        name="gateup_hist",
    )(h, w_gate, w_up, conv_w, cb, h1, h2)


def _down_kernel(a_ref, w_ref, x_ref, o_ref):
    o_ref[...] = x_ref[...] + jnp.dot(a_ref[...], w_ref[...], preferred_element_type=F32)


def _down(act, w_down, layer, x, tm):
    rows, n = act.shape
    d = x.shape[1]
    tn = min(COL_TILE // 2, d)
    return pl.pallas_call(
        _down_kernel,
        out_shape=jax.ShapeDtypeStruct((rows, d), F32),
        grid=(rows // tm, d // tn),
        in_specs=[pl.BlockSpec((tm, n), lambda i, j: (i, 0)),
                  pl.BlockSpec((None, n, tn), lambda i, j: (layer, 0, j)),
                  pl.BlockSpec((tm, tn), lambda i, j: (i, j))],
        out_specs=pl.BlockSpec((tm, tn), lambda i, j: (i, j)),
        compiler_params=_params("parallel", "arbitrary"),
        name="down",
    )(act, w_down, x)


def _sb_logs(z, mask):
    neg_abs = lax.bitcast_convert_type(lax.bitcast_convert_type(z, jnp.int32) | SIGN_BIT, F32)
    sp = jnp.maximum(z, 0.0) + jnp.log2(1.0 + jnp.exp2(neg_abs))
    return z - sp, (sp if mask is None else jnp.where(mask, sp, 0.0))


def _sb_split(drop):
    hi = drop.astype(BF16)
    return jnp.concatenate([hi, (drop - hi.astype(F32)).astype(BF16)], axis=0)


def _sb_cumsum(hi_lo, u):
    both = jnp.dot(hi_lo, u, preferred_element_type=F32)
    n = both.shape[0] // 2
    return both[0:n] + both[n:]


def _sb_prompt_kernel(aq_ref, aj_ref, bq_ref, dq_ref, dj_ref, q_ref, k_ref, v_ref, u_ref, o_ref,
                      qb, kb, vb, acc, car, z_s, beta_s, hl_s, w_s,
                      *, scale, t_len, n_steps):
    blk = SB_BLOCK
    t_pad = kb.shape[0]
    nq = t_pad // blk
    nt = (((1,), (1,)), ((), ()))

    for src, dst, mul in ((q_ref, qb, scale), (k_ref, kb, None), (v_ref, vb, None)):
        x = src[0] if mul is None else src[0] * mul
        dst[0:t_len, :] = x.astype(BF16)
        if t_pad > t_len:
            dst[t_len:t_pad, :] = jnp.zeros((t_pad - t_len, HEAD_DIM), BF16)
    for ref in (z_s, beta_s, hl_s, w_s):
        ref[...] = jnp.zeros_like(ref)

    row = lax.broadcasted_iota(jnp.int32, (blk, blk), 0)
    col = lax.broadcasted_iota(jnp.int32, (blk, blk), 1)
    causal = col < row

    def rows(i):
        return pl.ds(pl.multiple_of(i * blk, blk), blk)

    def diag(blocks):
        rs = [rows(qi) for qi in blocks]
        logs = [_sb_logs(lax.dot_general(qb[r, :], kb[r, :], nt, preferred_element_type=F32), causal) for r in rs]
        gaps = [_sb_cumsum(_sb_split(drop), u_ref[...]) for _, drop in logs]
        for qi, r, (beta, drop), gap in zip(blocks, rs, logs, gaps):
            w = jnp.where(causal, jnp.exp2(beta - gap), 0.0)
            acc[qi] = jnp.dot(w.astype(BF16), vb[r, :], preferred_element_type=F32)
            car[qi] = jnp.sum(drop, axis=-1, keepdims=True)

    def diag_pair(i, c):
        diag([2 * i, 2 * i + 1])
        return c

    lax.fori_loop(0, nq // 2, diag_pair, 0)
    if nq % 2:
        diag([nq - 1])

    def step(t, slot):
        other = 1 - slot
        acc[dq_ref[t]] += jnp.dot(w_s[other], vb[rows(dj_ref[t]), :], preferred_element_type=F32)
        w_s[slot] = jnp.exp2(beta_s[slot] - _sb_cumsum(hl_s[slot], u_ref[...])).astype(BF16)
        z_s[slot] = lax.dot_general(qb[rows(aq_ref[t]), :], kb[rows(aj_ref[t]), :], nt,
                                    preferred_element_type=F32)
        q1 = bq_ref[t]
        beta, drop = _sb_logs(z_s[other], None)
        carry = car[q1]
        beta_s[other] = beta - carry
        hl_s[other] = _sb_split(drop)
        car[q1] = carry + jnp.sum(drop, axis=-1, keepdims=True)

    def steps(it, c):
        for s in range(SB_UNROLL):
            step(it * SB_UNROLL + s, s % 2)
        return c

    acc[nq] = jnp.zeros((blk, HEAD_DIM), F32)
    car[nq] = jnp.zeros((blk, 1), F32)
    lax.fori_loop(0, n_steps // SB_UNROLL, steps, 0)
    for qi in range(nq):
        r = slice(qi * blk, min((qi + 1) * blk, t_len))
        o_ref[0, r, :] = acc[qi, 0:r.stop - r.start, :].astype(o_ref.dtype)


def _tri_ones(n):
    r = lax.broadcasted_iota(jnp.int32, (n, n), 0)
    c = lax.broadcasted_iota(jnp.int32, (n, n), 1)
    return (r > c).astype(BF16)


def _sb_prompt(proj, n_heads):
    b, t, _ = proj.shape
    blk = SB_BLOCK
    nq = pl.cdiv(t, blk)
    t_pad = nq * blk
    scale = LOG2_E / math.sqrt(HEAD_DIM)
    items = [(qi, j) for qi in range(1, nq) for j in range(qi - 1, -1, -1)]
    n_steps = 0 if not items else -(-(len(items) + 3) // SB_UNROLL) * SB_UNROLL
    item = lambda m: items[m] if 0 <= m < len(items) else None
    table = lambda delay, pick, spare: jnp.asarray(
        [spare if item(s - delay) is None else item(s - delay)[pick] for s in range(n_steps)] + [spare], jnp.int32)
    tables = (table(0, 0, 0), table(0, 1, 0), table(1, 0, nq), table(3, 0, nq), table(3, 1, 0))
    head = lambda off: pl.BlockSpec((1, t, HEAD_DIM), lambda bi, h, *_: (bi, 0, off + h))
    ring = lambda dt: pltpu.VMEM((2, blk, blk), dt)
    return pl.pallas_call(
        functools.partial(_sb_prompt_kernel, scale=scale, t_len=t, n_steps=n_steps),
        out_shape=jax.ShapeDtypeStruct((b, t, n_heads * HEAD_DIM), BF16),
        grid_spec=pltpu.PrefetchScalarGridSpec(
            num_scalar_prefetch=5,
            grid=(b, n_heads),
            in_specs=[head(0), head(n_heads), head(2 * n_heads),
                      pl.BlockSpec((blk, blk), lambda bi, h, *_: (0, 0))],
            out_specs=pl.BlockSpec((1, t, HEAD_DIM), lambda bi, h, *_: (bi, 0, h)),
            scratch_shapes=[pltpu.VMEM((t_pad, HEAD_DIM), BF16)] * 3
                           + [pltpu.VMEM((nq + 1, blk, HEAD_DIM), F32), pltpu.VMEM((nq + 1, blk, 1), F32),
                              ring(F32), ring(F32), pltpu.VMEM((2, 2 * blk, blk), BF16), ring(BF16)]),
        compiler_params=_params("parallel", "parallel"),
        name="sb_prompt",
    )(*tables, proj, proj, proj, _tri_ones(blk))


def _sb_sample_kernel(q_ref, kn_ref, vn_ref, kc_ref, vc_ref, u_ref, o_ref, *, scale, n_cache_blocks):
    blk = SB_BLOCK
    n = q_ref.shape[1]
    nt = (((1,), (1,)), ((), ()))
    row = lax.broadcasted_iota(jnp.int32, (n, n), 0)
    col = lax.broadcasted_iota(jnp.int32, (n, n), 1)
    masks = [col < row] + [None] * n_cache_blocks
    tris = [u_ref[0:n, 0:n]] + [u_ref[...]] * n_cache_blocks
    for h in range(q_ref.shape[2] // HEAD_DIM):
        sl = slice(h * HEAD_DIM, (h + 1) * HEAD_DIM)
        q = (q_ref[0, :, sl] * scale).astype(BF16)
        keys = [kn_ref[0, :, sl].astype(BF16)] + [kc_ref[0, j * blk:(j + 1) * blk, sl].astype(BF16)
                                                  for j in reversed(range(n_cache_blocks))]
        vals = [vn_ref[0, :, sl].astype(BF16)] + [vc_ref[0, j * blk:(j + 1) * blk, sl].astype(BF16)
                                                  for j in reversed(range(n_cache_blocks))]
        logs = [_sb_logs(lax.dot_general(q, k, nt, preferred_element_type=F32), m) for k, m in zip(keys, masks)]
        gaps = [_sb_cumsum(_sb_split(drop), u) for (_, drop), u in zip(logs, tris)]
        carry = jnp.zeros((n, 1), F32)
        acc = jnp.zeros((n, HEAD_DIM), F32)
        for (beta, drop), gap, v, m in zip(logs, gaps, vals, masks):
            w = jnp.exp2(beta - gap - carry)
            if m is not None:
                w = jnp.where(m, w, 0.0)
            acc = acc + jnp.dot(w.astype(BF16), v, preferred_element_type=F32)
            carry = carry + jnp.sum(drop, axis=-1, keepdims=True)
        o_ref[0, :, sl] = acc.astype(o_ref.dtype)


def _sb_sample(proj, cache_k, cache_v, layer, n_heads):
    b, n, _ = proj.shape
    p = cache_k.shape[2]
    blk = SB_BLOCK
    assert p % blk == 0 and n <= blk
    scale = LOG2_E / math.sqrt(HEAD_DIM)
    group = SB_SAMPLE_HEADS if n_heads % SB_SAMPLE_HEADS == 0 else 1
    width = group * HEAD_DIM
    n_groups = n_heads // group
    new_spec = lambda off: pl.BlockSpec((1, n, width), lambda bi, h: (bi, 0, off + h))
    cache_spec = pl.BlockSpec((None, 1, p, width), lambda bi, h: (layer, bi, 0, h))
    return pl.pallas_call(
        functools.partial(_sb_sample_kernel, scale=scale, n_cache_blocks=p // blk),
        out_shape=jax.ShapeDtypeStruct((b, n, n_heads * HEAD_DIM), BF16),
        grid=(b, n_groups),
        in_specs=[new_spec(0), new_spec(n_groups), new_spec(2 * n_groups), cache_spec, cache_spec,
                  pl.BlockSpec((blk, blk), lambda bi, h: (0, 0))],
        out_specs=pl.BlockSpec((1, n, width), lambda bi, h: (bi, 0, h)),
        compiler_params=_params("parallel", "parallel"),
        name="sb_sample",
    )(proj, proj, proj, cache_k, cache_v, _tri_ones(blk))


def _pool_kernel(x_ref, hist_ref, w_ref, scale_ref, o_ref, ext_ref, *, n_hist, group_dim):
    ti = pl.program_id(1)
    tm = x_ref.shape[1]
    pad = POOL_HIST + 1

    @pl.when(ti == 0)
    def _():
        ext_ref[0:pad, :] = hist_ref[0]

    @pl.when(ti != 0)
    def _():
        ext_ref[0:pad, :] = ext_ref[tm:tm + pad, :]

    ext_ref[pad:pad + tm, :] = x_ref[0]
    t = ti * tm + lax.broadcasted_iota(jnp.int32, (tm, 1), 0)
    for gi, win in enumerate(POOL_WINDOWS):
        cols = slice(gi * group_dim, (gi + 1) * group_dim)
        x = ext_ref[pad:pad + tm, cols]
        total = x
        for d in range(1, win):
            total = total + ext_ref[pad - d:pad - d + tm, cols]
        count = jnp.minimum(t + 1 + n_hist, win).astype(F32)
        pooled = total / count - x
        y = jnp.dot(pooled.astype(BF16), w_ref[gi], preferred_element_type=F32)
        o_ref[0, :, cols] = (y * scale_ref[:, cols]).astype(o_ref.dtype)


def _pool(proj, col_block, width, hist, n_hist, w_pool, scale, tm):
    b, t, _ = proj.shape
    groups = len(POOL_WINDOWS)
    group_dim = width // groups
    pad = POOL_HIST + 1
    hist_pad = jnp.concatenate([jnp.zeros((b, 1, width), F32), hist], axis=1)
    return pl.pallas_call(
        functools.partial(_pool_kernel, n_hist=n_hist, group_dim=group_dim),
        out_shape=jax.ShapeDtypeStruct((b, t, width), BF16),
        grid=(b, t // tm),
        in_specs=[pl.BlockSpec((1, tm, width), lambda bi, ti: (bi, ti, col_block)),
                  pl.BlockSpec((1, pad, width), lambda bi, ti: (bi, 0, 0)),
                  pl.BlockSpec((groups, group_dim, group_dim), lambda bi, ti: (0, 0, 0)),
                  pl.BlockSpec((1, width), lambda bi, ti: (0, 0))],
        out_specs=pl.BlockSpec((1, tm, width), lambda bi, ti: (bi, ti, 0)),
        scratch_shapes=[pltpu.VMEM((tm + pad, width), F32)],
        compiler_params=_params("parallel", "arbitrary"),
        name="pool",
    )(proj, hist_pad, w_pool, scale.reshape(1, width))


def _cumsum_rows(x):
    row = lax.broadcasted_iota(jnp.int32, x.shape, 0)
    shift = 1
    while shift < x.shape[0]:
        x = x + jnp.where(row >= shift, pltpu.roll(x, shift, 0), 0.0)
        shift *= 2
    return x


def _hgrn_kernel(q_ref, f_ref, i_ref, g_ref, la_ref, lc_ref, ng_ref, ones_ref, s0_ref, o_ref, sout_ref,
                 s_ref, ck_ref, vs_ref, *, n_heads, n_chunks):
    tb = pl.program_id(1)
    half = HG_CHUNK // 2
    hd = HEAD_DIM

    @pl.when(tb == 0)
    def _():
        s_ref[...] = s0_ref[0]

    la = la_ref[...]
    lc = lc_ref[...]
    ng = ng_ref[...]
    row8 = lax.broadcasted_iota(jnp.int32, (half, hd), 0)

    def chunk(c, carry):
        r0 = pl.multiple_of(c * HG_CHUNK, HG_CHUNK)
        qc = q_ref[0, pl.ds(r0, HG_CHUNK), :]
        fc = f_ref[0, pl.ds(r0, HG_CHUNK), :]
        gc = g_ref[0, pl.ds(r0, HG_CHUNK), :]
        q = qc * _sigmoid(qc)
        gate = gc * _sigmoid(gc)
        l1 = jnp.log(1.0 + jnp.exp(-jnp.abs(fc)))
        log_sig = jnp.minimum(fc, 0.0) - l1
        log_sig_neg = jnp.minimum(-fc, 0.0) - l1
        b = lc + log_sig
        log_f = jnp.maximum(la, b) + jnp.log(1.0 + jnp.exp(-jnp.abs(la - b)))
        log2_k = (lc + log_sig_neg) * LOG2_E
        cum = _cumsum_rows(log_f * LOG2_E)
        last = cum[HG_CHUNK - 1:HG_CHUNK, :]
        qt = (q * jnp.exp2(cum)).astype(BF16)
        kt = jnp.exp2(log2_k + last - cum).astype(BF16)
        dec = jnp.exp2(last)
        ck_ref[...] = cum - log2_k
        v = i_ref[0, pl.ds(r0, HG_CHUNK), :]
        vs_ref[...] = v
        vb = v.astype(BF16)
        for h in range(n_heads):
            sl = slice(h * hd, (h + 1) * hd)
            cum_a, cum_b = cum[0:half, sl], cum[half:, sl]
            q_a, q_b = q[0:half, sl], q[half:, sl]
            pairs = []
            for s in range(HG_CHUNK):
                cs = jnp.broadcast_to(ck_ref[s:s + 1, sl], (half, hd))
                if s < half:
                    pairs.append(q_a * jnp.where(row8 >= s, jnp.exp2(cum_a - cs), 0.0))
                    pairs.append(q_b * jnp.exp2(cum_b - cs))
                else:
                    pairs.append(q_b * jnp.where(row8 + half >= s, jnp.exp2(cum_b - cs), 0.0))
            scores = jnp.dot(jnp.concatenate(pairs, axis=0).astype(BF16), ones_ref[...],
                             preferred_element_type=F32)
            o_a = jnp.zeros((half, hd), F32)
            o_b = jnp.zeros((half, hd), F32)
            blocks = iter(range(len(pairs)))
            for s in range(HG_CHUNK):
                vs = jnp.broadcast_to(vs_ref[s:s + 1, sl], (half, hd))
                if s < half:
                    i = next(blocks)
                    o_a = o_a + scores[i * half:(i + 1) * half] * vs
                i = next(blocks)
                o_b = o_b + scores[i * half:(i + 1) * half] * vs
            st = s_ref[h]
            inter = lax.dot_general(qt[:, sl], st.astype(BF16), (((1,), (1,)), ((), ())),
                                    preferred_element_type=F32)
            o = jnp.concatenate([o_a, o_b], axis=0) + inter
            ms = jnp.mean(o * o, axis=-1, keepdims=True)
            o = o * lax.rsqrt(ms + EPS) * ng * gate[:, sl]
            o_ref[0, pl.ds(r0, HG_CHUNK), sl] = o.astype(o_ref.dtype)
            s_ref[h] = st * dec[:, sl] + lax.dot_general(vb[:, sl], kt[:, sl], (((0,), (0,)), ((), ())),
                                                         preferred_element_type=F32)
        return carry

    lax.fori_loop(0, n_chunks, chunk, 0, unroll=max(u for u in (1, 2, 3) if n_chunks % u == 0))

    @pl.when(tb == pl.num_programs(1) - 1)
    def _():
        sout_ref[0] = s_ref[...]


def _hgrn(proj, first_block, width, log_lb, log_1m_lb, norm_g, s0_t, tb_rows):
    b, t, _ = proj.shape
    n_heads = width // HEAD_DIM
    assert t % tb_rows == 0 and tb_rows % HG_CHUNK == 0
    col = lambda off: pl.BlockSpec((1, tb_rows, width), lambda bi, ti: (bi, ti, first_block + off))
    vec = pl.BlockSpec((1, width), lambda bi, ti: (0, 0))
    state_spec = pl.BlockSpec((1, n_heads, HEAD_DIM, HEAD_DIM), lambda bi, ti: (bi, 0, 0, 0))
    return pl.pallas_call(
        functools.partial(_hgrn_kernel, n_heads=n_heads, n_chunks=tb_rows // HG_CHUNK),
        out_shape=(jax.ShapeDtypeStruct((b, t, width), BF16),
                   jax.ShapeDtypeStruct((b, n_heads, HEAD_DIM, HEAD_DIM), F32)),
        grid=(b, t // tb_rows),
        in_specs=[col(0), col(1), col(2), col(3), vec, vec,
                  pl.BlockSpec((1, HEAD_DIM), lambda bi, ti: (0, 0)),
                  pl.BlockSpec((HEAD_DIM, HEAD_DIM), lambda bi, ti: (0, 0)), state_spec],
        out_specs=(pl.BlockSpec((1, tb_rows, width), lambda bi, ti: (bi, ti, 0)), state_spec),
        scratch_shapes=[pltpu.VMEM((n_heads, HEAD_DIM, HEAD_DIM), F32), pltpu.VMEM((HG_CHUNK, width), F32),
                        pltpu.VMEM((HG_CHUNK, width), F32)],
        compiler_params=_params("parallel", "arbitrary"),
        name="hgrn",
    )(proj, proj, proj, proj, log_lb.reshape(1, width), log_1m_lb.reshape(1, width),
      norm_g.reshape(1, HEAD_DIM).astype(F32), jnp.ones((HEAD_DIM, HEAD_DIM), BF16), s0_t)


def _layer(x, batch, seq, lw, layer, sb_cache, pool_hist, n_hist, s0, conv_hist, kv_stack=None):
    rows, d = x.shape
    sb_width = d // 4
    pool_width = d // 4
    hg_width = d // 2
    n_sb_heads = sb_width // HEAD_DIM
    tm = _row_tile(seq, MAX_ROW_TILE) if conv_hist is None else rows
    assert rows % tm == 0

    h = _rmsnorm(x, lw["norm1_g"], BF16)
    proj = _matmul(h, lw["w_in"], layer, tm)
    proj3 = proj.reshape(batch, seq, 3 * d)
    if sb_cache is None:
        a_out = _sb_prompt(proj3, n_sb_heads)
    else:
        a_out = _sb_sample(proj3, sb_cache[0], sb_cache[1], layer, n_sb_heads)
    seq_tile = _row_tile(seq, MAX_ROW_TILE)
    b_out = _pool(proj3, 3, pool_width, pool_hist, n_hist, lw["pool_w"], lw["pool_scale"], seq_tile)
    c_out, s_new_t = _hgrn(proj3, 2, hg_width, lw["log_lb"], lw["log_1m_lb"], lw["hgrn_norm_g"],
                           jnp.swapaxes(s0, -1, -2), _row_tile(seq, MAX_HGRN_ROWS))
    x = _outproj(a_out.reshape(rows, sb_width), b_out.reshape(rows, pool_width),
                 c_out.reshape(rows, hg_width), lw["w_out"], layer, x, tm)

    h2 = _rmsnorm(x, lw["norm2_g"], BF16)
    act, g_tail = _gateup(h2, lw["w_gate"], lw["w_up"], layer, lw["conv_w"], lw["conv_b"], tm, seq, conv_hist)
    x = _down(act, lw["w_down"], layer, x, tm)
    conv_state = g_tail[:, 8 - (CONV_W - 1):, :]

    if kv_stack is None:
        k_new = proj3[:, :, sb_width:2 * sb_width].reshape(batch, seq, n_sb_heads, HEAD_DIM)
        v_new = proj3[:, :, 2 * sb_width:3 * sb_width].reshape(batch, seq, n_sb_heads, HEAD_DIM)
    else:
        n_layers, k_stack, v_stack = kv_stack
        k_new = _to_heads(proj3, 1, n_sb_heads, k_stack, n_layers, layer, seq_tile)
        v_new = _to_heads(proj3, 2, n_sb_heads, v_stack, n_layers, layer, seq_tile)
    xb = proj3[:, :, 3 * sb_width:3 * sb_width + pool_width]
    pool_new = jnp.concatenate([pool_hist, xb], axis=1)[:, -POOL_HIST:]
    s_new = jnp.swapaxes(s_new_t, -1, -2)
    return x, (k_new, v_new, s_new, pool_new, conv_state)


def kernel(x_prompt, x_sample, cache_sb_k, cache_sb_v, state_hgrn, state_pool, state_conv, meta_tokens,
           norm1_g, w_in, pool_w, pool_scale, hgrn_lower_bounds, hgrn_norm_g, w_out, norm2_g, ffn_w_gate,
           ffn_w_up, ffn_conv_w, ffn_conv_b, ffn_w_down, final_norm_g):
    bp, seq_p, d = x_prompt.shape
    bs, seq_s, _ = x_sample.shape
    depth = w_in.shape[0]
    tp = N_META + seq_p
    hg_heads = (d // 2) // HEAD_DIM

    meta = jnp.broadcast_to(meta_tokens[None], (bp, N_META, d))
    xp = jnp.concatenate([meta, x_prompt], axis=1).reshape(bp * tp, d)
    xs = x_sample.reshape(bs * seq_s, d)

    probs = jax.nn.softmax(hgrn_lower_bounds.astype(F32), axis=0)
    lower = jnp.maximum(jnp.cumsum(probs, axis=0) - probs[0], 0.0)
    log_lb = jnp.log(lower)
    log_1m_lb = jnp.log1p(-lower)

    w_in_b, w_out_b = w_in.astype(BF16), w_out.astype(BF16)
    w_gate_b, w_up_b, w_down_b = ffn_w_gate.astype(BF16), ffn_w_up.astype(BF16), ffn_w_down.astype(BF16)
    pool_w_b = pool_w.astype(BF16)
    cache = (cache_sb_k.reshape(depth, bs, -1, d // 4), cache_sb_v.reshape(depth, bs, -1, d // 4))

    outs_p, outs_s = [], []
    pk = pv = None
    for l in range(depth):
        lw = dict(norm1_g=norm1_g[l], w_in=w_in_b, pool_w=pool_w_b[l], pool_scale=pool_scale[l],
                  log_lb=log_lb[l], log_1m_lb=log_1m_lb[l], hgrn_norm_g=hgrn_norm_g[l], w_out=w_out_b,
                  norm2_g=norm2_g[l], w_gate=w_gate_b, w_up=w_up_b, conv_w=ffn_conv_w[l], conv_b=ffn_conv_b[l],
                  w_down=w_down_b)
        xp, out = _layer(xp, bp, tp, lw, l, None, jnp.zeros((bp, POOL_HIST, d // 4), F32), 0,
                         jnp.zeros((bp, hg_heads, HEAD_DIM, HEAD_DIM), F32), None, (depth, pk, pv))
        pk, pv = out[0], out[1]
        outs_p.append(out)
        xs, out = _layer(xs, bs, seq_s, lw, l, cache, state_pool[l], POOL_HIST, state_hgrn[l], state_conv[l])
        outs_s.append(out)

    y_prompt = _rmsnorm_skip(xp.reshape(bp, tp, d), final_norm_g, N_META, F32)
    y_sample = _rmsnorm(xs, final_norm_g, F32).reshape(bs, seq_s, d)
    stack = lambda outs, idx: jnp.stack([o[idx] for o in outs])
    return (y_prompt, y_sample,
            pk, pv, stack(outs_p, 2), stack(outs_p, 3), stack(outs_p, 4),
            stack(outs_s, 0), stack(outs_s, 1), stack(outs_s, 2), stack(outs_s, 3), stack(outs_s, 4))
```

```python
import functools
import math

import jax
import jax.numpy as jnp
from jax import lax
from jax.experimental import pallas as pl
from jax.experimental.pallas import tpu as pltpu

F32 = jnp.float32
BF16 = jnp.bfloat16

N_META = 16
EPS = 1e-6
HEAD_DIM = 128
POOL_WINDOWS = (2, 4, 8, 16)
POOL_HIST = max(POOL_WINDOWS) - 1
CONV_W = 3

VMEM_LIMIT_BYTES = 56 * 1024 * 1024
BF16_SUBLANES = 16
MAX_ROW_TILE = 1024
MAX_NORM_ROWS = 512
MAX_HGRN_ROWS = 256
SB_BLOCK = 256
SB_SAMPLE_HEADS = 2
SB_UNROLL = 6
HG_CHUNK = 16
COL_TILE = 512
LANE = 128
SIGN_BIT = -2 ** 31
LOG2_E = 1.4426950408889634


def _row_tile(rows, limit):
    best = None
    for t in range(BF16_SUBLANES, min(rows, limit) + 1, BF16_SUBLANES):
        if rows % t == 0:
            best = t
    assert best is not None, rows
    return best


def _params(*semantics):
    return pltpu.CompilerParams(dimension_semantics=semantics, vmem_limit_bytes=VMEM_LIMIT_BYTES)


def _sigmoid(x):
    return 1.0 / (1.0 + jnp.exp(-x))


def _rmsnorm_kernel(x_ref, g_ref, o_ref):
    x = x_ref[...]
    ms = jnp.mean(x * x, axis=-1, keepdims=True)
    o_ref[...] = (x * lax.rsqrt(ms + EPS) * g_ref[...]).astype(o_ref.dtype)


def _rmsnorm_skip(x, g, skip, out_dtype):
    b, t, d = x.shape
    tr = _row_tile(t - skip, MAX_NORM_ROWS)
    assert skip % 8 == 0
    return pl.pallas_call(
        _rmsnorm_kernel,
        out_shape=jax.ShapeDtypeStruct((b, t - skip, d), out_dtype),
        grid=(b, (t - skip) // tr),
        in_specs=[pl.BlockSpec((pl.Element(1), pl.Element(tr), pl.Element(d)),
                               lambda bi, i: (bi, pl.multiple_of(skip + i * tr, 8), 0)),
                  pl.BlockSpec((1, 1, d), lambda bi, i: (0, 0, 0))],
        out_specs=pl.BlockSpec((1, tr, d), lambda bi, i: (bi, i, 0)),
        compiler_params=_params("parallel", "parallel"),
        name="rmsnorm_skip",
    )(x, g.reshape(1, 1, d).astype(F32))


def _rmsnorm(x, g, out_dtype):
    rows, d = x.shape
    tr = _row_tile(rows, MAX_NORM_ROWS)
    return pl.pallas_call(
        _rmsnorm_kernel,
        out_shape=jax.ShapeDtypeStruct((rows, d), out_dtype),
        grid=(rows // tr,),
        in_specs=[pl.BlockSpec((tr, d), lambda i: (i, 0)),
                  pl.BlockSpec((1, d), lambda i: (0, 0))],
        out_specs=pl.BlockSpec((tr, d), lambda i: (i, 0)),
        compiler_params=_params("parallel"),
        name="rmsnorm",
    )(x, g.reshape(1, d).astype(F32))


def _matmul_kernel(x_ref, w_ref, o_ref):
    o_ref[...] = jnp.dot(x_ref[...], w_ref[...], preferred_element_type=F32)


def _matmul(x, w, layer, tm):
    rows, k = x.shape
    n = w.shape[2]
    tn = min(2 * COL_TILE, n)
    return pl.pallas_call(
        _matmul_kernel,
        out_shape=jax.ShapeDtypeStruct((rows, n), F32),
        grid=(pl.cdiv(n, tn), rows // tm),
        in_specs=[pl.BlockSpec((tm, k), lambda j, i: (i, 0)),
                  pl.BlockSpec((None, k, tn), lambda j, i: (layer, 0, j))],
        out_specs=pl.BlockSpec((tm, tn), lambda j, i: (i, j)),
        compiler_params=_params("parallel", "parallel"),
        name="matmul",
    )(x, w)


def _heads_kernel(x_ref, *refs, n_heads):
    o_ref = refs[-1]
    for h in range(n_heads):
        o_ref[0, 0, :, h, :] = x_ref[0, :, h * HEAD_DIM:(h + 1) * HEAD_DIM]


def _to_heads(proj, col_block, n_heads, stacked, n_layers, layer, tm):
    b, t, _ = proj.shape
    shape = (n_layers, b, t, n_heads, HEAD_DIM)
    prev = () if stacked is None else (stacked,)
    return pl.pallas_call(
        functools.partial(_heads_kernel, n_heads=n_heads),
        out_shape=jax.ShapeDtypeStruct(shape, F32),
        grid=(b, t // tm),
        in_specs=[pl.BlockSpec((1, tm, n_heads * HEAD_DIM), lambda bi, i: (bi, i, col_block))]
                 + [pl.BlockSpec(memory_space=pl.ANY)] * len(prev),
        out_specs=pl.BlockSpec((1, 1, tm, n_heads, HEAD_DIM), lambda bi, i: (layer, bi, i, 0, 0)),
        input_output_aliases={1: 0} if prev else {},
        compiler_params=_params("parallel", "parallel"),
        name="to_heads",
    )(proj, *prev)


def _outproj_kernel(a_ref, b_ref, c_ref, wa_ref, wb_ref, wc_ref, x_ref, o_ref):
    acc = jnp.dot(a_ref[...], wa_ref[...], preferred_element_type=F32)
    acc += jnp.dot(b_ref[...], wb_ref[...], preferred_element_type=F32)
    acc += jnp.dot(c_ref[...], wc_ref[...], preferred_element_type=F32)
    o_ref[...] = x_ref[...] + acc


def _outproj(a, b, c, w_out, layer, x, tm):
    rows, d = x.shape
    wa, wb, wc = a.shape[1], b.shape[1], c.shape[1]
    assert wa == wb and wc == 2 * wa and w_out.shape[1] == wa + wb + wc
    tn = min(2 * COL_TILE, d)
    return pl.pallas_call(
        _outproj_kernel,
        out_shape=jax.ShapeDtypeStruct((rows, d), F32),
        grid=(d // tn, rows // tm),
        in_specs=[pl.BlockSpec((tm, wa), lambda j, i: (i, 0)),
                  pl.BlockSpec((tm, wb), lambda j, i: (i, 0)),
                  pl.BlockSpec((tm, wc), lambda j, i: (i, 0)),
                  pl.BlockSpec((None, wa, tn), lambda j, i: (layer, 0, j)),
                  pl.BlockSpec((None, wb, tn), lambda j, i: (layer, 1, j)),
                  pl.BlockSpec((None, wc, tn), lambda j, i: (layer, 1, j)),
                  pl.BlockSpec((tm, tn), lambda j, i: (i, j))],
        out_specs=pl.BlockSpec((tm, tn), lambda j, i: (i, j)),
        compiler_params=_params("parallel", "parallel"),
        name="outproj",
    )(a, b, c, w_out, w_out, w_out, x)


def _ffn_act(g, g1, g2, u, cw_ref, cb_ref):
    conv = cb_ref[...] + g2 * cw_ref[0:1, :] + g1 * cw_ref[1:2, :] + g * cw_ref[2:3, :]
    return (conv * _sigmoid(conv) * u).astype(BF16)


def _gateup_carry_kernel(h_ref, wg_ref, wu_ref, cw_ref, cb_ref, o_ref, tail_ref, g_ref, *, tiles_per_seq):
    i = pl.program_id(1)
    tm = h_ref.shape[0]

    @pl.when(i % tiles_per_seq == 0)
    def _():
        g_ref[0:8, :] = jnp.zeros((8, g_ref.shape[1]), F32)

    @pl.when(i % tiles_per_seq != 0)
    def _():
        g_ref[0:8, :] = g_ref[tm:tm + 8, :]

    h = h_ref[...]
    g_ref[8:tm + 8, :] = jnp.dot(h, wg_ref[...], preferred_element_type=F32)
    u = jnp.dot(h, wu_ref[...], preferred_element_type=F32)
    o_ref[...] = _ffn_act(g_ref[8:tm + 8, :], g_ref[7:tm + 7, :], g_ref[6:tm + 6, :], u, cw_ref, cb_ref)
    tail_ref[0] = g_ref[tm:tm + 8, :]


def _gateup_hist_kernel(h_ref, wg_ref, wu_ref, cw_ref, cb_ref, h1_ref, h2_ref, o_ref, tail_ref, *, seq_len):
    h = h_ref[...]
    g = jnp.dot(h, wg_ref[...], preferred_element_type=F32)
    u = jnp.dot(h, wu_ref[...], preferred_element_type=F32)
    pos = lax.broadcasted_iota(jnp.int32, g.shape, 0) % seq_len
    g1 = jnp.where(pos == 0, h1_ref[...], pltpu.roll(g, 1, 0))
    g2 = jnp.where(pos < 2, h2_ref[...], pltpu.roll(g, 2, 0))
    o_ref[...] = _ffn_act(g, g1, g2, u, cw_ref, cb_ref)
    tail_ref[...] = g.reshape(g.shape[0] // seq_len, seq_len, g.shape[1])[:, seq_len - 8:, :]


def _gateup(h, w_gate, w_up, layer, conv_w, conv_b, tm, seq_len, hist):
    rows, d = h.shape
    n = w_gate.shape[2]
    tn = min(COL_TILE, n)
    grid = (pl.cdiv(n, tn), rows // tm)
    row_spec = pl.BlockSpec((tm, d), lambda j, i: (i, 0))
    w_spec = pl.BlockSpec((None, d, tn), lambda j, i: (layer, 0, j))
    cw_spec = pl.BlockSpec((CONV_W, tn), lambda j, i: (0, j))
    cb_spec = pl.BlockSpec((1, tn), lambda j, i: (0, j))
    out_spec = pl.BlockSpec((tm, tn), lambda j, i: (i, j))
    out_shape = (jax.ShapeDtypeStruct((rows, n), BF16), jax.ShapeDtypeStruct((rows // seq_len, 8, n), F32))
    cb = conv_b.reshape(1, n)
    if hist is None:
        assert seq_len % tm == 0
        return pl.pallas_call(
            functools.partial(_gateup_carry_kernel, tiles_per_seq=seq_len // tm),
            out_shape=out_shape, grid=grid,
            in_specs=[row_spec, w_spec, w_spec, cw_spec, cb_spec],
            out_specs=(out_spec, pl.BlockSpec((1, 8, tn), lambda j, i: (i // (seq_len // tm), 0, j))),
            scratch_shapes=[pltpu.VMEM((tm + 8, tn), F32)],
            compiler_params=_params("arbitrary", "arbitrary"),
            name="gateup_carry",
        )(h, w_gate, w_up, conv_w, cb)
    assert tm % seq_len == 0
    nb = rows // seq_len
    zeros = jnp.zeros((nb, seq_len, n), F32)
    h1 = zeros.at[:, 0].set(hist[:, 1]).reshape(rows, n)
    h2 = zeros.at[:, 0].set(hist[:, 0]).at[:, 1].set(hist[:, 1]).reshape(rows, n)
    hist_spec = pl.BlockSpec((tm, tn), lambda j, i: (i, j))
    return pl.pallas_call(
        functools.partial(_gateup_hist_kernel, seq_len=seq_len),
        out_shape=out_shape, grid=grid,
        in_specs=[row_spec, w_spec, w_spec, cw_spec, cb_spec, hist_spec, hist_spec],
        out_specs=(out_spec, pl.BlockSpec((tm // seq_len, 8, tn), lambda j, i: (i, 0, j))),
        compiler_params=_params("parallel", "parallel"),
        name="gateup_hist",
    )(h, w_gate, w_up, conv_w, cb, h1, h2)


def _down_kernel(a_ref, w_ref, x_ref, o_ref):
    o_ref[...] = x_ref[...] + jnp.dot(a_ref[...], w_ref[...], preferred_element_type=F32)


def _down(act, w_down, layer, x, tm):
    rows, n = act.shape
    d = x.shape[1]
    tn = min(COL_TILE // 2, d)
    return pl.pallas_call(
        _down_kernel,
        out_shape=jax.ShapeDtypeStruct((rows, d), F32),
        grid=(rows // tm, d // tn),
        in_specs=[pl.BlockSpec((tm, n), lambda i, j: (i, 0)),
                  pl.BlockSpec((None, n, tn), lambda i, j: (layer, 0, j)),
                  pl.BlockSpec((tm, tn), lambda i, j: (i, j))],
        out_specs=pl.BlockSpec((tm, tn), lambda i, j: (i, j)),
        compiler_params=_params("parallel", "arbitrary"),
        name="down",
    )(act, w_down, x)


def _sb_logs(z, mask):
    neg_abs = lax.bitcast_convert_type(lax.bitcast_convert_type(z, jnp.int32) | SIGN_BIT, F32)
    sp = jnp.maximum(z, 0.0) + jnp.log2(1.0 + jnp.exp2(neg_abs))
    return z - sp, (sp if mask is None else jnp.where(mask, sp, 0.0))


def _sb_split(drop):
    hi = drop.astype(BF16)
    return hi, (drop - hi.astype(F32)).astype(BF16)


def _sb_cumsum_stacked(drop, u):
    both = jnp.dot(jnp.concatenate(_sb_split(drop), axis=0), u, preferred_element_type=F32)
    n = drop.shape[0]
    return both[0:n] + both[n:]


def _sb_prompt_kernel(aq_ref, aj_ref, bq_ref, dq_ref, dj_ref, q_ref, k_ref, v_ref, u_ref, o_ref,
                      qb, kb, vb, acc, car, z_s, beta_s, hi_s, lo_s, w_s,
                      *, scale, t_len, n_steps):
    blk = SB_BLOCK
    t_pad = kb.shape[0]
    nq = t_pad // blk
    nt = (((1,), (1,)), ((), ()))

    for src, dst, mul in ((q_ref, qb, scale), (k_ref, kb, None), (v_ref, vb, None)):
        x = src[0] if mul is None else src[0] * mul
        dst[0:t_len, :] = x.astype(BF16)
        if t_pad > t_len:
            dst[t_len:t_pad, :] = jnp.zeros((t_pad - t_len, HEAD_DIM), BF16)
    for ref in (z_s, beta_s, hi_s, lo_s, w_s):
        ref[...] = jnp.zeros_like(ref)

    row = lax.broadcasted_iota(jnp.int32, (blk, blk), 0)
    col = lax.broadcasted_iota(jnp.int32, (blk, blk), 1)
    causal = col < row

    def rows(i):
        return pl.ds(pl.multiple_of(i * blk, blk), blk)

    def cumsum(hi, lo):
        return (jnp.dot(hi, u_ref[...], preferred_element_type=F32)
                + jnp.dot(lo, u_ref[...], preferred_element_type=F32))

    def diag(blocks):
        rs = [rows(qi) for qi in blocks]
        logs = [_sb_logs(lax.dot_general(qb[r, :], kb[r, :], nt, preferred_element_type=F32), causal) for r in rs]
        gaps = [cumsum(*_sb_split(drop)) for _, drop in logs]
        for qi, r, (beta, drop), gap in zip(blocks, rs, logs, gaps):
            w = jnp.where(causal, jnp.exp2(beta - gap), 0.0)
            acc[qi] = jnp.dot(w.astype(BF16), vb[r, :], preferred_element_type=F32)
            car[qi] = jnp.sum(drop, axis=-1, keepdims=True)

    def diag_pair(i, c):
        diag([2 * i, 2 * i + 1])
        return c

    lax.fori_loop(0, nq // 2, diag_pair, 0)
    if nq % 2:
        diag([nq - 1])

    def step(t, slot):
        other = 1 - slot
        acc[dq_ref[t]] += jnp.dot(w_s[other], vb[rows(dj_ref[t]), :], preferred_element_type=F32)
        w_s[slot] = jnp.exp2(beta_s[slot] - cumsum(hi_s[slot], lo_s[slot])).astype(BF16)
        z_s[slot] = lax.dot_general(qb[rows(aq_ref[t]), :], kb[rows(aj_ref[t]), :], nt,
                                    preferred_element_type=F32)
        q1 = bq_ref[t]
        beta, drop = _sb_logs(z_s[other], None)
        carry = car[q1]
        beta_s[other] = beta - carry
        hi_s[other], lo_s[other] = _sb_split(drop)
        car[q1] = carry + jnp.sum(drop, axis=-1, keepdims=True)

    def steps(it, c):
        for s in range(SB_UNROLL):
            step(it * SB_UNROLL + s, s % 2)
        return c

    acc[nq] = jnp.zeros((blk, HEAD_DIM), F32)
    car[nq] = jnp.zeros((blk, 1), F32)
    lax.fori_loop(0, n_steps // SB_UNROLL, steps, 0)
    for qi in range(nq):
        r = slice(qi * blk, min((qi + 1) * blk, t_len))
        o_ref[0, r, :] = acc[qi, 0:r.stop - r.start, :].astype(o_ref.dtype)


def _tri_ones(n):
    r = lax.broadcasted_iota(jnp.int32, (n, n), 0)
    c = lax.broadcasted_iota(jnp.int32, (n, n), 1)
    return (r > c).astype(BF16)


def _sb_prompt(proj, n_heads):
    b, t, _ = proj.shape
    blk = SB_BLOCK
    nq = pl.cdiv(t, blk)
    t_pad = nq * blk
    scale = LOG2_E / math.sqrt(HEAD_DIM)
    items = [(qi, j) for qi in range(1, nq) for j in range(qi - 1, -1, -1)]
    n_steps = 0 if not items else -(-(len(items) + 3) // SB_UNROLL) * SB_UNROLL
    item = lambda m: items[m] if 0 <= m < len(items) else None
    table = lambda delay, pick, spare: jnp.asarray(
        [spare if item(s - delay) is None else item(s - delay)[pick] for s in range(n_steps)] + [spare], jnp.int32)
    tables = (table(0, 0, 0), table(0, 1, 0), table(1, 0, nq), table(3, 0, nq), table(3, 1, 0))
    head = lambda off: pl.BlockSpec((1, t, HEAD_DIM), lambda bi, h, *_: (bi, 0, off + h))
    ring = lambda dt: pltpu.VMEM((2, blk, blk), dt)
    return pl.pallas_call(
        functools.partial(_sb_prompt_kernel, scale=scale, t_len=t, n_steps=n_steps),
        out_shape=jax.ShapeDtypeStruct((b, t, n_heads * HEAD_DIM), BF16),
        grid_spec=pltpu.PrefetchScalarGridSpec(
            num_scalar_prefetch=5,
            grid=(b, n_heads),
            in_specs=[head(0), head(n_heads), head(2 * n_heads),
                      pl.BlockSpec((blk, blk), lambda bi, h, *_: (0, 0))],
            out_specs=pl.BlockSpec((1, t, HEAD_DIM), lambda bi, h, *_: (bi, 0, h)),
            scratch_shapes=[pltpu.VMEM((t_pad, HEAD_DIM), BF16)] * 3
                           + [pltpu.VMEM((nq + 1, blk, HEAD_DIM), F32), pltpu.VMEM((nq + 1, blk, 1), F32),
                              ring(F32), ring(F32), ring(BF16), ring(BF16), ring(BF16)]),
        compiler_params=_params("parallel", "parallel"),
        name="sb_prompt",
    )(*tables, proj, proj, proj, _tri_ones(blk))


def _sb_sample_kernel(q_ref, kn_ref, vn_ref, kc_ref, vc_ref, u_ref, o_ref, *, scale, n_cache_blocks):
    blk = SB_BLOCK
    n = q_ref.shape[1]
    nt = (((1,), (1,)), ((), ()))
    row = lax.broadcasted_iota(jnp.int32, (n, n), 0)
    col = lax.broadcasted_iota(jnp.int32, (n, n), 1)
    masks = [col < row] + [None] * n_cache_blocks
    tris = [u_ref[0:n, 0:n]] + [u_ref[...]] * n_cache_blocks
    for h in range(q_ref.shape[2] // HEAD_DIM):
        sl = slice(h * HEAD_DIM, (h + 1) * HEAD_DIM)
        q = (q_ref[0, :, sl] * scale).astype(BF16)
        keys = [kn_ref[0, :, sl].astype(BF16)] + [kc_ref[0, j * blk:(j + 1) * blk, sl].astype(BF16)
                                                  for j in reversed(range(n_cache_blocks))]
        vals = [vn_ref[0, :, sl].astype(BF16)] + [vc_ref[0, j * blk:(j + 1) * blk, sl].astype(BF16)
                                                  for j in reversed(range(n_cache_blocks))]
        logs = [_sb_logs(lax.dot_general(q, k, nt, preferred_element_type=F32), m) for k, m in zip(keys, masks)]
        gaps = [_sb_cumsum_stacked(drop, u) for (_, drop), u in zip(logs, tris)]
        carry = jnp.zeros((n, 1), F32)
        acc = jnp.zeros((n, HEAD_DIM), F32)
        for (beta, drop), gap, v, m in zip(logs, gaps, vals, masks):
            w = jnp.exp2(beta - gap - carry)
            if m is not None:
                w = jnp.where(m, w, 0.0)
            acc = acc + jnp.dot(w.astype(BF16), v, preferred_element_type=F32)
            carry = carry + jnp.sum(drop, axis=-1, keepdims=True)
        o_ref[0, :, sl] = acc.astype(o_ref.dtype)


def _sb_sample(proj, cache_k, cache_v, layer, n_heads):
    b, n, _ = proj.shape
    p = cache_k.shape[2]
    blk = SB_BLOCK
    assert p % blk == 0 and n <= blk
    scale = LOG2_E / math.sqrt(HEAD_DIM)
    group = SB_SAMPLE_HEADS if n_heads % SB_SAMPLE_HEADS == 0 else 1
    width = group * HEAD_DIM
    n_groups = n_heads // group
    new_spec = lambda off: pl.BlockSpec((1, n, width), lambda bi, h: (bi, 0, off + h))
    cache_spec = pl.BlockSpec((None, 1, p, width), lambda bi, h: (layer, bi, 0, h))
    return pl.pallas_call(
        functools.partial(_sb_sample_kernel, scale=scale, n_cache_blocks=p // blk),
        out_shape=jax.ShapeDtypeStruct((b, n, n_heads * HEAD_DIM), BF16),
        grid=(b, n_groups),
        in_specs=[new_spec(0), new_spec(n_groups), new_spec(2 * n_groups), cache_spec, cache_spec,
                  pl.BlockSpec((blk, blk), lambda bi, h: (0, 0))],
        out_specs=pl.BlockSpec((1, n, width), lambda bi, h: (bi, 0, h)),
        compiler_params=_params("parallel", "parallel"),
        name="sb_sample",
    )(proj, proj, proj, cache_k, cache_v, _tri_ones(blk))


def _pool_kernel(x_ref, hist_ref, w_ref, scale_ref, o_ref, ext_ref, *, n_hist, group_dim):
    ti = pl.program_id(1)
    tm = x_ref.shape[1]
    pad = POOL_HIST + 1

    @pl.when(ti == 0)
    def _():
        ext_ref[0:pad, :] = hist_ref[0]

    @pl.when(ti != 0)
    def _():
        ext_ref[0:pad, :] = ext_ref[tm:tm + pad, :]

    ext_ref[pad:pad + tm, :] = x_ref[0]
    t = ti * tm + lax.broadcasted_iota(jnp.int32, (tm, 1), 0)
    for gi, win in enumerate(POOL_WINDOWS):
        cols = slice(gi * group_dim, (gi + 1) * group_dim)
        x = ext_ref[pad:pad + tm, cols]
        total = x
        for d in range(1, win):
            total = total + ext_ref[pad - d:pad - d + tm, cols]
        count = jnp.minimum(t + 1 + n_hist, win).astype(F32)
        pooled = total / count - x
        y = jnp.dot(pooled.astype(BF16), w_ref[gi], preferred_element_type=F32)
        o_ref[0, :, cols] = (y * scale_ref[:, cols]).astype(o_ref.dtype)


def _pool(proj, col_block, width, hist, n_hist, w_pool, scale, tm):
    b, t, _ = proj.shape
    groups = len(POOL_WINDOWS)
    group_dim = width // groups
    pad = POOL_HIST + 1
    hist_pad = jnp.concatenate([jnp.zeros((b, 1, width), F32), hist], axis=1)
    return pl.pallas_call(
        functools.partial(_pool_kernel, n_hist=n_hist, group_dim=group_dim),
        out_shape=jax.ShapeDtypeStruct((b, t, width), BF16),
        grid=(b, t // tm),
        in_specs=[pl.BlockSpec((1, tm, width), lambda bi, ti: (bi, ti, col_block)),
                  pl.BlockSpec((1, pad, width), lambda bi, ti: (bi, 0, 0)),
                  pl.BlockSpec((groups, group_dim, group_dim), lambda bi, ti: (0, 0, 0)),
                  pl.BlockSpec((1, width), lambda bi, ti: (0, 0))],
        out_specs=pl.BlockSpec((1, tm, width), lambda bi, ti: (bi, ti, 0)),
        scratch_shapes=[pltpu.VMEM((tm + pad, width), F32)],
        compiler_params=_params("parallel", "arbitrary"),
        name="pool",
    )(proj, hist_pad, w_pool, scale.reshape(1, width))


def _cumsum_rows(x):
    row = lax.broadcasted_iota(jnp.int32, x.shape, 0)
    shift = 1
    while shift < x.shape[0]:
        x = x + jnp.where(row >= shift, pltpu.roll(x, shift, 0), 0.0)
        shift *= 2
    return x


def _hgrn_kernel(q_ref, f_ref, i_ref, g_ref, la_ref, lc_ref, ng_ref, ones_ref, s0_ref, o_ref, sout_ref,
                 s_ref, ck_ref, vs_ref, *, n_heads, n_chunks):
    tb = pl.program_id(1)
    half = HG_CHUNK // 2
    hd = HEAD_DIM

    @pl.when(tb == 0)
    def _():
        s_ref[...] = s0_ref[0]

    la = la_ref[...]
    lc = lc_ref[...]
    ng = ng_ref[...]
    row8 = lax.broadcasted_iota(jnp.int32, (half, hd), 0)

    def chunk(c, carry):
        r0 = pl.multiple_of(c * HG_CHUNK, HG_CHUNK)
        qc = q_ref[0, pl.ds(r0, HG_CHUNK), :]
        fc = f_ref[0, pl.ds(r0, HG_CHUNK), :]
        gc = g_ref[0, pl.ds(r0, HG_CHUNK), :]
        q = qc * _sigmoid(qc)
        gate = gc * _sigmoid(gc)
        l1 = jnp.log(1.0 + jnp.exp(-jnp.abs(fc)))
        log_sig = jnp.minimum(fc, 0.0) - l1
        log_sig_neg = jnp.minimum(-fc, 0.0) - l1
        b = lc + log_sig
        log_f = jnp.maximum(la, b) + jnp.log(1.0 + jnp.exp(-jnp.abs(la - b)))
        log2_k = (lc + log_sig_neg) * LOG2_E
        cum = _cumsum_rows(log_f * LOG2_E)
        last = cum[HG_CHUNK - 1:HG_CHUNK, :]
        qt = (q * jnp.exp2(cum)).astype(BF16)
        kt = jnp.exp2(log2_k + last - cum).astype(BF16)
        dec = jnp.exp2(last)
        ck_ref[...] = cum - log2_k
        v = i_ref[0, pl.ds(r0, HG_CHUNK), :]
        vs_ref[...] = v
        vb = v.astype(BF16)
        for h in range(n_heads):
            sl = slice(h * hd, (h + 1) * hd)
            cum_a, cum_b = cum[0:half, sl], cum[half:, sl]
            q_a, q_b = q[0:half, sl], q[half:, sl]
            pairs = []
            for s in range(HG_CHUNK):
                cs = jnp.broadcast_to(ck_ref[s:s + 1, sl], (half, hd))
                if s < half:
                    pairs.append(q_a * jnp.where(row8 >= s, jnp.exp2(cum_a - cs), 0.0))
                    pairs.append(q_b * jnp.exp2(cum_b - cs))
                else:
                    pairs.append(q_b * jnp.where(row8 + half >= s, jnp.exp2(cum_b - cs), 0.0))
            scores = jnp.dot(jnp.concatenate(pairs, axis=0).astype(BF16), ones_ref[...],
                             preferred_element_type=F32)
            o_a = jnp.zeros((half, hd), F32)
            o_b = jnp.zeros((half, hd), F32)
            blocks = iter(range(len(pairs)))
            for s in range(HG_CHUNK):
                vs = jnp.broadcast_to(vs_ref[s:s + 1, sl], (half, hd))
                if s < half:
                    i = next(blocks)
                    o_a = o_a + scores[i * half:(i + 1) * half] * vs
                i = next(blocks)
                o_b = o_b + scores[i * half:(i + 1) * half] * vs
            st = s_ref[h]
            inter = lax.dot_general(qt[:, sl], st.astype(BF16), (((1,), (1,)), ((), ())),
                                    preferred_element_type=F32)
            o = jnp.concatenate([o_a, o_b], axis=0) + inter
            ms = jnp.mean(o * o, axis=-1, keepdims=True)
            o = o * lax.rsqrt(ms + EPS) * ng * gate[:, sl]
            o_ref[0, pl.ds(r0, HG_CHUNK), sl] = o.astype(o_ref.dtype)
            s_ref[h] = st * dec[:, sl] + lax.dot_general(vb[:, sl], kt[:, sl], (((0,), (0,)), ((), ())),
                                                         preferred_element_type=F32)
        return carry

    lax.fori_loop(0, n_chunks, chunk, 0, unroll=max(u for u in (1, 2, 3) if n_chunks % u == 0))

    @pl.when(tb == pl.num_programs(1) - 1)
    def _():
        sout_ref[0] = s_ref[...]


def _hgrn(proj, first_block, width, log_lb, log_1m_lb, norm_g, s0_t, tb_rows):
    b, t, _ = proj.shape
    n_heads = width // HEAD_DIM
    assert t % tb_rows == 0 and tb_rows % HG_CHUNK == 0
    col = lambda off: pl.BlockSpec((1, tb_rows, width), lambda bi, ti: (bi, ti, first_block + off))
    vec = pl.BlockSpec((1, width), lambda bi, ti: (0, 0))
    state_spec = pl.BlockSpec((1, n_heads, HEAD_DIM, HEAD_DIM), lambda bi, ti: (bi, 0, 0, 0))
    return pl.pallas_call(
        functools.partial(_hgrn_kernel, n_heads=n_heads, n_chunks=tb_rows // HG_CHUNK),
        out_shape=(jax.ShapeDtypeStruct((b, t, width), BF16),
                   jax.ShapeDtypeStruct((b, n_heads, HEAD_DIM, HEAD_DIM), F32)),
        grid=(b, t // tb_rows),
        in_specs=[col(0), col(1), col(2), col(3), vec, vec,
                  pl.BlockSpec((1, HEAD_DIM), lambda bi, ti: (0, 0)),
                  pl.BlockSpec((HEAD_DIM, HEAD_DIM), lambda bi, ti: (0, 0)), state_spec],
        out_specs=(pl.BlockSpec((1, tb_rows, width), lambda bi, ti: (bi, ti, 0)), state_spec),
        scratch_shapes=[pltpu.VMEM((n_heads, HEAD_DIM, HEAD_DIM), F32), pltpu.VMEM((HG_CHUNK, width), F32),
                        pltpu.VMEM((HG_CHUNK, width), F32)],
        compiler_params=_params("parallel", "arbitrary"),
        name="hgrn",
    )(proj, proj, proj, proj, log_lb.reshape(1, width), log_1m_lb.reshape(1, width),
      norm_g.reshape(1, HEAD_DIM).astype(F32), jnp.ones((HEAD_DIM, HEAD_DIM), BF16), s0_t)


def _layer(x, batch, seq, lw, layer, sb_cache, pool_hist, n_hist, s0, conv_hist, kv_stack=None):
    rows, d = x.shape
    sb_width = d // 4
    pool_width = d // 4
    hg_width = d // 2
    n_sb_heads = sb_width // HEAD_DIM
    tm = _row_tile(seq, MAX_ROW_TILE) if conv_hist is None else rows
    assert rows % tm == 0

    h = _rmsnorm(x, lw["norm1_g"], BF16)
    proj = _matmul(h, lw["w_in"], layer, tm)
    proj3 = proj.reshape(batch, seq, 3 * d)
    if sb_cache is None:
        a_out = _sb_prompt(proj3, n_sb_heads)
    else:
        a_out = _sb_sample(proj3, sb_cache[0], sb_cache[1], layer, n_sb_heads)
    seq_tile = _row_tile(seq, MAX_ROW_TILE)
    b_out = _pool(proj3, 3, pool_width, pool_hist, n_hist, lw["pool_w"], lw["pool_scale"], seq_tile)
    c_out, s_new_t = _hgrn(proj3, 2, hg_width, lw["log_lb"], lw["log_1m_lb"], lw["hgrn_norm_g"],
                           jnp.swapaxes(s0, -1, -2), _row_tile(seq, MAX_HGRN_ROWS))
    x = _outproj(a_out.reshape(rows, sb_width), b_out.reshape(rows, pool_width),
                 c_out.reshape(rows, hg_width), lw["w_out"], layer, x, tm)

    h2 = _rmsnorm(x, lw["norm2_g"], BF16)
    act, g_tail = _gateup(h2, lw["w_gate"], lw["w_up"], layer, lw["conv_w"], lw["conv_b"], tm, seq, conv_hist)
    x = _down(act, lw["w_down"], layer, x, tm)
    conv_state = g_tail[:, 8 - (CONV_W - 1):, :]

    if kv_stack is None:
        k_new = proj3[:, :, sb_width:2 * sb_width].reshape(batch, seq, n_sb_heads, HEAD_DIM)
        v_new = proj3[:, :, 2 * sb_width:3 * sb_width].reshape(batch, seq, n_sb_heads, HEAD_DIM)
    else:
        n_layers, k_stack, v_stack = kv_stack
        k_new = _to_heads(proj3, 1, n_sb_heads, k_stack, n_layers, layer, seq_tile)
        v_new = _to_heads(proj3, 2, n_sb_heads, v_stack, n_layers, layer, seq_tile)
    xb = proj3[:, :, 3 * sb_width:3 * sb_width + pool_width]
    pool_new = jnp.concatenate([pool_hist, xb], axis=1)[:, -POOL_HIST:]
    s_new = jnp.swapaxes(s_new_t, -1, -2)
    return x, (k_new, v_new, s_new, pool_new, conv_state)


def kernel(x_prompt, x_sample, cache_sb_k, cache_sb_v, state_hgrn, state_pool, state_conv, meta_tokens,
           norm1_g, w_in, pool_w, pool_scale, hgrn_lower_bounds, hgrn_norm_g, w_out, norm2_g, ffn_w_gate,
           ffn_w_up, ffn_conv_w, ffn_conv_b, ffn_w_down, final_norm_g):
    bp, seq_p, d = x_prompt.shape
    bs, seq_s, _ = x_sample.shape
    depth = w_in.shape[0]
    tp = N_META + seq_p
    hg_heads = (d // 2) // HEAD_DIM

    meta = jnp.broadcast_to(meta_tokens[None], (bp, N_META, d))
    xp = jnp.concatenate([meta, x_prompt], axis=1).reshape(bp * tp, d)
    xs = x_sample.reshape(bs * seq_s, d)

    probs = jax.nn.softmax(hgrn_lower_bounds.astype(F32), axis=0)
    lower = jnp.maximum(jnp.cumsum(probs, axis=0) - probs[0], 0.0)
    log_lb = jnp.log(lower)
    log_1m_lb = jnp.log1p(-lower)

    w_in_b, w_out_b = w_in.astype(BF16), w_out.astype(BF16)
    w_gate_b, w_up_b, w_down_b = ffn_w_gate.astype(BF16), ffn_w_up.astype(BF16), ffn_w_down.astype(BF16)
    pool_w_b = pool_w.astype(BF16)
    cache = (cache_sb_k.reshape(depth, bs, -1, d // 4), cache_sb_v.reshape(depth, bs, -1, d // 4))

    outs_p, outs_s = [], []
    pk = pv = None
    for l in range(depth):
        lw = dict(norm1_g=norm1_g[l], w_in=w_in_b, pool_w=pool_w_b[l], pool_scale=pool_scale[l],
                  log_lb=log_lb[l], log_1m_lb=log_1m_lb[l], hgrn_norm_g=hgrn_norm_g[l], w_out=w_out_b,
                  norm2_g=norm2_g[l], w_gate=w_gate_b, w_up=w_up_b, conv_w=ffn_conv_w[l], conv_b=ffn_conv_b[l],
                  w_down=w_down_b)
        xp, out = _layer(xp, bp, tp, lw, l, None, jnp.zeros((bp, POOL_HIST, d // 4), F32), 0,
                         jnp.zeros((bp, hg_heads, HEAD_DIM, HEAD_DIM), F32), None, (depth, pk, pv))
        pk, pv = out[0], out[1]
        outs_p.append(out)
        xs, out = _layer(xs, bs, seq_s, lw, l, cache, state_pool[l], POOL_HIST, state_hgrn[l], state_conv[l])
        outs_s.append(out)

    y_prompt = _rmsnorm_skip(xp.reshape(bp, tp, d), final_norm_g, N_META, F32)
    y_sample = _rmsnorm(xs, final_norm_g, F32).reshape(bs, seq_s, d)
    stack = lambda outs, idx: jnp.stack([o[idx] for o in outs])
    return (y_prompt, y_sample,
            pk, pv, stack(outs_p, 2), stack(outs_p, 3), stack(outs_p, 4),
            stack(outs_s, 0), stack(outs_s, 1), stack(outs_s, 2), stack(outs_s, 3), stack(outs_s, 4))
```

```python
import functools
import math

import jax
import jax.numpy as jnp
from jax import lax
from jax.experimental import pallas as pl
from jax.experimental.pallas import tpu as pltpu

F32 = jnp.float32
BF16 = jnp.bfloat16

N_META = 16
EPS = 1e-6
HEAD_DIM = 128
POOL_WINDOWS = (2, 4, 8, 16)
POOL_HIST = max(POOL_WINDOWS) - 1
CONV_W = 3

VMEM_LIMIT_BYTES = 56 * 1024 * 1024
BF16_SUBLANES = 16
MAX_ROW_TILE = 1024
MAX_NORM_ROWS = 512
MAX_HGRN_ROWS = 256
SB_BLOCK = 256
SB_SAMPLE_HEADS = 2
SB_UNROLL = 6
HG_CHUNK = 16
COL_TILE = 512
LANE = 128
SIGN_BIT = -2 ** 31
LOG2_E = 1.4426950408889634


def _row_tile(rows, limit):
    best = None
    for t in range(BF16_SUBLANES, min(rows, limit) + 1, BF16_SUBLANES):
        if rows % t == 0:
            best = t
    assert best is not None, rows
    return best


def _params(*semantics):
    return pltpu.CompilerParams(dimension_semantics=semantics, vmem_limit_bytes=VMEM_LIMIT_BYTES)


def _sigmoid(x):
    return 1.0 / (1.0 + jnp.exp(-x))


def _rmsnorm_kernel(x_ref, g_ref, o_ref):
    x = x_ref[...]
    ms = jnp.mean(x * x, axis=-1, keepdims=True)
    o_ref[...] = (x * lax.rsqrt(ms + EPS) * g_ref[...]).astype(o_ref.dtype)


def _rmsnorm_skip(x, g, skip, out_dtype):
    b, t, d = x.shape
    tr = _row_tile(t - skip, MAX_NORM_ROWS)
    assert skip % 8 == 0
    return pl.pallas_call(
        _rmsnorm_kernel,
        out_shape=jax.ShapeDtypeStruct((b, t - skip, d), out_dtype),
        grid=(b, (t - skip) // tr),
        in_specs=[pl.BlockSpec((pl.Element(1), pl.Element(tr), pl.Element(d)),
                               lambda bi, i: (bi, pl.multiple_of(skip + i * tr, 8), 0)),
                  pl.BlockSpec((1, 1, d), lambda bi, i: (0, 0, 0))],
        out_specs=pl.BlockSpec((1, tr, d), lambda bi, i: (bi, i, 0)),
        compiler_params=_params("parallel", "parallel"),
        name="rmsnorm_skip",
    )(x, g.reshape(1, 1, d).astype(F32))


def _rmsnorm(x, g, out_dtype):
    rows, d = x.shape
    tr = _row_tile(rows, MAX_NORM_ROWS)
    return pl.pallas_call(
        _rmsnorm_kernel,
        out_shape=jax.ShapeDtypeStruct((rows, d), out_dtype),
        grid=(rows // tr,),
        in_specs=[pl.BlockSpec((tr, d), lambda i: (i, 0)),
                  pl.BlockSpec((1, d), lambda i: (0, 0))],
        out_specs=pl.BlockSpec((tr, d), lambda i: (i, 0)),
        compiler_params=_params("parallel"),
        name="rmsnorm",
    )(x, g.reshape(1, d).astype(F32))


def _matmul_kernel(x_ref, w_ref, o_ref):
    o_ref[...] = jnp.dot(x_ref[...], w_ref[...], preferred_element_type=F32)


def _matmul(x, w, layer, tm):
    rows, k = x.shape
    n = w.shape[2]
    tn = min(2 * COL_TILE, n)
    return pl.pallas_call(
        _matmul_kernel,
        out_shape=jax.ShapeDtypeStruct((rows, n), F32),
        grid=(pl.cdiv(n, tn), rows // tm),
        in_specs=[pl.BlockSpec((tm, k), lambda j, i: (i, 0)),
                  pl.BlockSpec((None, k, tn), lambda j, i: (layer, 0, j))],
        out_specs=pl.BlockSpec((tm, tn), lambda j, i: (i, j)),
        compiler_params=_params("parallel", "parallel"),
        name="matmul",
    )(x, w)


def _heads_kernel(x_ref, *refs, n_heads):
    o_ref = refs[-1]
    for h in range(n_heads):
        o_ref[0, 0, :, h, :] = x_ref[0, :, h * HEAD_DIM:(h + 1) * HEAD_DIM]


def _to_heads(proj, col_block, n_heads, stacked, n_layers, layer, tm):
    b, t, _ = proj.shape
    shape = (n_layers, b, t, n_heads, HEAD_DIM)
    prev = () if stacked is None else (stacked,)
    return pl.pallas_call(
        functools.partial(_heads_kernel, n_heads=n_heads),
        out_shape=jax.ShapeDtypeStruct(shape, F32),
        grid=(b, t // tm),
        in_specs=[pl.BlockSpec((1, tm, n_heads * HEAD_DIM), lambda bi, i: (bi, i, col_block))]
                 + [pl.BlockSpec(memory_space=pl.ANY)] * len(prev),
        out_specs=pl.BlockSpec((1, 1, tm, n_heads, HEAD_DIM), lambda bi, i: (layer, bi, i, 0, 0)),
        input_output_aliases={1: 0} if prev else {},
        compiler_params=_params("parallel", "parallel"),
        name="to_heads",
    )(proj, *prev)


def _outproj_kernel(a_ref, b_ref, c_ref, wa_ref, wb_ref, wc_ref, x_ref, o_ref):
    acc = jnp.dot(a_ref[...], wa_ref[...], preferred_element_type=F32)
    acc += jnp.dot(b_ref[...], wb_ref[...], preferred_element_type=F32)
    acc += jnp.dot(c_ref[...], wc_ref[...], preferred_element_type=F32)
    o_ref[...] = x_ref[...] + acc


def _outproj(a, b, c, w_out, layer, x, tm):
    rows, d = x.shape
    wa, wb, wc = a.shape[1], b.shape[1], c.shape[1]
    assert wa == wb and wc == 2 * wa and w_out.shape[1] == wa + wb + wc
    tn = min(2 * COL_TILE, d)
    return pl.pallas_call(
        _outproj_kernel,
        out_shape=jax.ShapeDtypeStruct((rows, d), F32),
        grid=(d // tn, rows // tm),
        in_specs=[pl.BlockSpec((tm, wa), lambda j, i: (i, 0)),
                  pl.BlockSpec((tm, wb), lambda j, i: (i, 0)),
                  pl.BlockSpec((tm, wc), lambda j, i: (i, 0)),
                  pl.BlockSpec((None, wa, tn), lambda j, i: (layer, 0, j)),
                  pl.BlockSpec((None, wb, tn), lambda j, i: (layer, 1, j)),
                  pl.BlockSpec((None, wc, tn), lambda j, i: (layer, 1, j)),
                  pl.BlockSpec((tm, tn), lambda j, i: (i, j))],
        out_specs=pl.BlockSpec((tm, tn), lambda j, i: (i, j)),
        compiler_params=_params("parallel", "parallel"),
        name="outproj",
    )(a, b, c, w_out, w_out, w_out, x)


def _ffn_act(g, g1, g2, u, cw_ref, cb_ref):
    conv = cb_ref[...] + g2 * cw_ref[0:1, :] + g1 * cw_ref[1:2, :] + g * cw_ref[2:3, :]
    return (conv * _sigmoid(conv) * u).astype(BF16)


def _gateup_carry_kernel(h_ref, wg_ref, wu_ref, cw_ref, cb_ref, o_ref, tail_ref, g_ref, *, tiles_per_seq):
    i = pl.program_id(1)
    tm = h_ref.shape[0]

    @pl.when(i % tiles_per_seq == 0)
    def _():
        g_ref[0:8, :] = jnp.zeros((8, g_ref.shape[1]), F32)

    @pl.when(i % tiles_per_seq != 0)
    def _():
        g_ref[0:8, :] = g_ref[tm:tm + 8, :]

    h = h_ref[...]
    g_ref[8:tm + 8, :] = jnp.dot(h, wg_ref[...], preferred_element_type=F32)
    u = jnp.dot(h, wu_ref[...], preferred_element_type=F32)
    o_ref[...] = _ffn_act(g_ref[8:tm + 8, :], g_ref[7:tm + 7, :], g_ref[6:tm + 6, :], u, cw_ref, cb_ref)
    tail_ref[0] = g_ref[tm:tm + 8, :]


def _gateup_hist_kernel(h_ref, wg_ref, wu_ref, cw_ref, cb_ref, h1_ref, h2_ref, o_ref, tail_ref, *, seq_len):
    h = h_ref[...]
    g = jnp.dot(h, wg_ref[...], preferred_element_type=F32)
    u = jnp.dot(h, wu_ref[...], preferred_element_type=F32)
    pos = lax.broadcasted_iota(jnp.int32, g.shape, 0) % seq_len
    g1 = jnp.where(pos == 0, h1_ref[...], pltpu.roll(g, 1, 0))
    g2 = jnp.where(pos < 2, h2_ref[...], pltpu.roll(g, 2, 0))
    o_ref[...] = _ffn_act(g, g1, g2, u, cw_ref, cb_ref)
    tail_ref[...] = g.reshape(g.shape[0] // seq_len, seq_len, g.shape[1])[:, seq_len - 8:, :]


def _gateup(h, w_gate, w_up, layer, conv_w, conv_b, tm, seq_len, hist):
    rows, d = h.shape
    n = w_gate.shape[2]
    tn = min(COL_TILE, n)
    grid = (pl.cdiv(n, tn), rows // tm)
    row_spec = pl.BlockSpec((tm, d), lambda j, i: (i, 0))
    w_spec = pl.BlockSpec((None, d, tn), lambda j, i: (layer, 0, j))
    cw_spec = pl.BlockSpec((CONV_W, tn), lambda j, i: (0, j))
    cb_spec = pl.BlockSpec((1, tn), lambda j, i: (0, j))
    out_spec = pl.BlockSpec((tm, tn), lambda j, i: (i, j))
    out_shape = (jax.ShapeDtypeStruct((rows, n), BF16), jax.ShapeDtypeStruct((rows // seq_len, 8, n), F32))
    cb = conv_b.reshape(1, n)
    if hist is None:
        assert seq_len % tm == 0
        return pl.pallas_call(
            functools.partial(_gateup_carry_kernel, tiles_per_seq=seq_len // tm),
            out_shape=out_shape, grid=grid,
            in_specs=[row_spec, w_spec, w_spec, cw_spec, cb_spec],
            out_specs=(out_spec, pl.BlockSpec((1, 8, tn), lambda j, i: (i // (seq_len // tm), 0, j))),
            scratch_shapes=[pltpu.VMEM((tm + 8, tn), F32)],
            compiler_params=_params("arbitrary", "arbitrary"),
            name="gateup_carry",
        )(h, w_gate, w_up, conv_w, cb)
    assert tm % seq_len == 0
    nb = rows // seq_len
    zeros = jnp.zeros((nb, seq_len, n), F32)
    h1 = zeros.at[:, 0].set(hist[:, 1]).reshape(rows, n)
    h2 = zeros.at[:, 0].set(hist[:, 0]).at[:, 1].set(hist[:, 1]).reshape(rows, n)
    hist_spec = pl.BlockSpec((tm, tn), lambda j, i: (i, j))
    return pl.pallas_call(
        functools.partial(_gateup_hist_kernel, seq_len=seq_len),
        out_shape=out_shape, grid=grid,
        in_specs=[row_spec, w_spec, w_spec, cw_spec, cb_spec, hist_spec, hist_spec],
        out_specs=(out_spec, pl.BlockSpec((tm // seq_len, 8, tn), lambda j, i: (i, 0, j))),
        compiler_params=_params("parallel", "parallel"),
        name="gateup_hist",
    )(h, w_gate, w_up, conv_w, cb, h1, h2)


def _down_kernel(a_ref, w_ref, x_ref, o_ref):
    o_ref[...] = x_ref[...] + jnp.dot(a_ref[...], w_ref[...], preferred_element_type=F32)


def _down(act, w_down, layer, x, tm):
    rows, n = act.shape
    d = x.shape[1]
    tn = min(COL_TILE // 2, d)

    def pipeline(a_hbm, w_hbm, x_hbm, o_hbm):
        pltpu.emit_pipeline(
            _down_kernel,
            grid=(rows // tm, d // tn),
            in_specs=[pl.BlockSpec((tm, n), lambda i, j: (i, 0), pipeline_mode=pl.Buffered(2, use_lookahead=True)),
                      pl.BlockSpec((None, n, tn), lambda i, j: (layer, 0, j)),
                      pl.BlockSpec((tm, tn), lambda i, j: (i, j))],
            out_specs=[pl.BlockSpec((tm, tn), lambda i, j: (i, j))],
        )(a_hbm, w_hbm, x_hbm, o_hbm)

    return pl.pallas_call(
        pipeline,
        out_shape=jax.ShapeDtypeStruct((rows, d), F32),
        in_specs=[pl.BlockSpec(memory_space=pl.ANY)] * 3,
        out_specs=pl.BlockSpec(memory_space=pl.ANY),
        compiler_params=pltpu.CompilerParams(vmem_limit_bytes=VMEM_LIMIT_BYTES),
        name="down",
    )(act, w_down, x)


def _sb_logs(z, mask):
    neg_abs = lax.bitcast_convert_type(lax.bitcast_convert_type(z, jnp.int32) | SIGN_BIT, F32)
    sp = jnp.maximum(z, 0.0) + jnp.log2(1.0 + jnp.exp2(neg_abs))
    return z - sp, (sp if mask is None else jnp.where(mask, sp, 0.0))


def _sb_split(drop):
    hi = drop.astype(BF16)
    return hi, (drop - hi.astype(F32)).astype(BF16)


def _sb_cumsum_stacked(drop, u):
    both = jnp.dot(jnp.concatenate(_sb_split(drop), axis=0), u, preferred_element_type=F32)
    n = drop.shape[0]
    return both[0:n] + both[n:]


def _sb_prompt_kernel(aq_ref, aj_ref, bq_ref, dq_ref, dj_ref, q_ref, k_ref, v_ref, u_ref, o_ref,
                      qb, kb, vb, acc, car, z_s, beta_s, hi_s, lo_s, w_s,
                      *, scale, t_len, n_steps):
    blk = SB_BLOCK
    t_pad = kb.shape[0]
    nq = t_pad // blk
    nt = (((1,), (1,)), ((), ()))

    for src, dst, mul in ((q_ref, qb, scale), (k_ref, kb, None), (v_ref, vb, None)):
        x = src[0] if mul is None else src[0] * mul
        dst[0:t_len, :] = x.astype(BF16)
        if t_pad > t_len:
            dst[t_len:t_pad, :] = jnp.zeros((t_pad - t_len, HEAD_DIM), BF16)
    for ref in (z_s, beta_s, hi_s, lo_s, w_s):
        ref[...] = jnp.zeros_like(ref)

    row = lax.broadcasted_iota(jnp.int32, (blk, blk), 0)
    col = lax.broadcasted_iota(jnp.int32, (blk, blk), 1)
    causal = col < row

    def rows(i):
        return pl.ds(pl.multiple_of(i * blk, blk), blk)

    def cumsum(hi, lo):
        return (jnp.dot(hi, u_ref[...], preferred_element_type=F32)
                + jnp.dot(lo, u_ref[...], preferred_element_type=F32))

    def diag(blocks):
        rs = [rows(qi) for qi in blocks]
        logs = [_sb_logs(lax.dot_general(qb[r, :], kb[r, :], nt, preferred_element_type=F32), causal) for r in rs]
        gaps = [cumsum(*_sb_split(drop)) for _, drop in logs]
        for qi, r, (beta, drop), gap in zip(blocks, rs, logs, gaps):
            w = jnp.where(causal, jnp.exp2(beta - gap), 0.0)
            acc[qi] = jnp.dot(w.astype(BF16), vb[r, :], preferred_element_type=F32)
            car[qi] = jnp.sum(drop, axis=-1, keepdims=True)

    def diag_pair(i, c):
        diag([2 * i, 2 * i + 1])
        return c

    lax.fori_loop(0, nq // 2, diag_pair, 0)
    if nq % 2:
        diag([nq - 1])

    def step(t, slot):
        other = 1 - slot
        acc[dq_ref[t]] += jnp.dot(w_s[other], vb[rows(dj_ref[t]), :], preferred_element_type=F32)
        w_s[slot] = jnp.exp2(beta_s[slot] - cumsum(hi_s[slot], lo_s[slot])).astype(BF16)
        z_s[slot] = lax.dot_general(qb[rows(aq_ref[t]), :], kb[rows(aj_ref[t]), :], nt,
                                    preferred_element_type=F32)
        q1 = bq_ref[t]
        beta, drop = _sb_logs(z_s[other], None)
        carry = car[q1]
        beta_s[other] = beta - carry
        hi_s[other], lo_s[other] = _sb_split(drop)
        car[q1] = carry + jnp.sum(drop, axis=-1, keepdims=True)

    def steps(it, c):
        for s in range(SB_UNROLL):
            step(it * SB_UNROLL + s, s % 2)
        return c

    acc[nq] = jnp.zeros((blk, HEAD_DIM), F32)
    car[nq] = jnp.zeros((blk, 1), F32)
    lax.fori_loop(0, n_steps // SB_UNROLL, steps, 0)
    for qi in range(nq):
        r = slice(qi * blk, min((qi + 1) * blk, t_len))
        o_ref[0, r, :] = acc[qi, 0:r.stop - r.start, :].astype(o_ref.dtype)


def _tri_ones(n):
    r = lax.broadcasted_iota(jnp.int32, (n, n), 0)
    c = lax.broadcasted_iota(jnp.int32, (n, n), 1)
    return (r > c).astype(BF16)


def _sb_prompt(proj, n_heads):
    b, t, _ = proj.shape
    blk = SB_BLOCK
    nq = pl.cdiv(t, blk)
    t_pad = nq * blk
    scale = LOG2_E / math.sqrt(HEAD_DIM)
    items = [(qi, j) for qi in range(1, nq) for j in range(qi - 1, -1, -1)]
    n_steps = 0 if not items else -(-(len(items) + 3) // SB_UNROLL) * SB_UNROLL
    item = lambda m: items[m] if 0 <= m < len(items) else None
    table = lambda delay, pick, spare: jnp.asarray(
        [spare if item(s - delay) is None else item(s - delay)[pick] for s in range(n_steps)] + [spare], jnp.int32)
    tables = (table(0, 0, 0), table(0, 1, 0), table(1, 0, nq), table(3, 0, nq), table(3, 1, 0))
    head = lambda off: pl.BlockSpec((1, t, HEAD_DIM), lambda bi, h, *_: (bi, 0, off + h))
    ring = lambda dt: pltpu.VMEM((2, blk, blk), dt)
    return pl.pallas_call(
        functools.partial(_sb_prompt_kernel, scale=scale, t_len=t, n_steps=n_steps),
        out_shape=jax.ShapeDtypeStruct((b, t, n_heads * HEAD_DIM), BF16),
        grid_spec=pltpu.PrefetchScalarGridSpec(
            num_scalar_prefetch=5,
            grid=(b, n_heads),
            in_specs=[head(0), head(n_heads), head(2 * n_heads),
                      pl.BlockSpec((blk, blk), lambda bi, h, *_: (0, 0))],
            out_specs=pl.BlockSpec((1, t, HEAD_DIM), lambda bi, h, *_: (bi, 0, h)),
            scratch_shapes=[pltpu.VMEM((t_pad, HEAD_DIM), BF16)] * 3
                           + [pltpu.VMEM((nq + 1, blk, HEAD_DIM), F32), pltpu.VMEM((nq + 1, blk, 1), F32),
                              ring(F32), ring(F32), ring(BF16), ring(BF16), ring(BF16)]),
        compiler_params=_params("parallel", "parallel"),
        name="sb_prompt",
    )(*tables, proj, proj, proj, _tri_ones(blk))


def _sb_sample_kernel(q_ref, kn_ref, vn_ref, kc_ref, vc_ref, u_ref, o_ref, *, scale, n_cache_blocks):
    blk = SB_BLOCK
    n = q_ref.shape[1]
    nt = (((1,), (1,)), ((), ()))
    row = lax.broadcasted_iota(jnp.int32, (n, n), 0)
    col = lax.broadcasted_iota(jnp.int32, (n, n), 1)
    masks = [col < row] + [None] * n_cache_blocks
    tris = [u_ref[0:n, 0:n]] + [u_ref[...]] * n_cache_blocks
    for h in range(q_ref.shape[2] // HEAD_DIM):
        sl = slice(h * HEAD_DIM, (h + 1) * HEAD_DIM)
        q = (q_ref[0, :, sl] * scale).astype(BF16)
        keys = [kn_ref[0, :, sl].astype(BF16)] + [kc_ref[0, j * blk:(j + 1) * blk, sl].astype(BF16)
                                                  for j in reversed(range(n_cache_blocks))]
        vals = [vn_ref[0, :, sl].astype(BF16)] + [vc_ref[0, j * blk:(j + 1) * blk, sl].astype(BF16)
                                                  for j in reversed(range(n_cache_blocks))]
        logs = [_sb_logs(lax.dot_general(q, k, nt, preferred_element_type=F32), m) for k, m in zip(keys, masks)]
        gaps = [_sb_cumsum_stacked(drop, u) for (_, drop), u in zip(logs, tris)]
        carry = jnp.zeros((n, 1), F32)
        acc = jnp.zeros((n, HEAD_DIM), F32)
        for (beta, drop), gap, v, m in zip(logs, gaps, vals, masks):
            w = jnp.exp2(beta - gap - carry)
            if m is not None:
                w = jnp.where(m, w, 0.0)
            acc = acc + jnp.dot(w.astype(BF16), v, preferred_element_type=F32)
            carry = carry + jnp.sum(drop, axis=-1, keepdims=True)
        o_ref[0, :, sl] = acc.astype(o_ref.dtype)


def _sb_sample(proj, cache_k, cache_v, layer, n_heads):
    b, n, _ = proj.shape
    p = cache_k.shape[2]
    blk = SB_BLOCK
    assert p % blk == 0 and n <= blk
    scale = LOG2_E / math.sqrt(HEAD_DIM)
    group = SB_SAMPLE_HEADS if n_heads % SB_SAMPLE_HEADS == 0 else 1
    width = group * HEAD_DIM
    n_groups = n_heads // group
    new_spec = lambda off: pl.BlockSpec((1, n, width), lambda bi, h: (bi, 0, off + h))
    cache_spec = pl.BlockSpec((None, 1, p, width), lambda bi, h: (layer, bi, 0, h))
    return pl.pallas_call(
        functools.partial(_sb_sample_kernel, scale=scale, n_cache_blocks=p // blk),
        out_shape=jax.ShapeDtypeStruct((b, n, n_heads * HEAD_DIM), BF16),
        grid=(b, n_groups),
        in_specs=[new_spec(0), new_spec(n_groups), new_spec(2 * n_groups), cache_spec, cache_spec,
                  pl.BlockSpec((blk, blk), lambda bi, h: (0, 0))],
        out_specs=pl.BlockSpec((1, n, width), lambda bi, h: (bi, 0, h)),
        compiler_params=_params("parallel", "parallel"),
        name="sb_sample",
    )(proj, proj, proj, cache_k, cache_v, _tri_ones(blk))


def _pool_kernel(x_ref, hist_ref, w_ref, scale_ref, o_ref, ext_ref, *, n_hist, group_dim):
    ti = pl.program_id(1)
    tm = x_ref.shape[1]
    pad = POOL_HIST + 1

    @pl.when(ti == 0)
    def _():
        ext_ref[0:pad, :] = hist_ref[0]

    @pl.when(ti != 0)
    def _():
        ext_ref[0:pad, :] = ext_ref[tm:tm + pad, :]

    ext_ref[pad:pad + tm, :] = x_ref[0]
    t = ti * tm + lax.broadcasted_iota(jnp.int32, (tm, 1), 0)
    for gi, win in enumerate(POOL_WINDOWS):
        cols = slice(gi * group_dim, (gi + 1) * group_dim)
        x = ext_ref[pad:pad + tm, cols]
        total = x
        for d in range(1, win):
            total = total + ext_ref[pad - d:pad - d + tm, cols]
        count = jnp.minimum(t + 1 + n_hist, win).astype(F32)
        pooled = total / count - x
        y = jnp.dot(pooled.astype(BF16), w_ref[gi], preferred_element_type=F32)
        o_ref[0, :, cols] = (y * scale_ref[:, cols]).astype(o_ref.dtype)


def _pool(proj, col_block, width, hist, n_hist, w_pool, scale, tm):
    b, t, _ = proj.shape
    groups = len(POOL_WINDOWS)
    group_dim = width // groups
    pad = POOL_HIST + 1
    hist_pad = jnp.concatenate([jnp.zeros((b, 1, width), F32), hist], axis=1)
    return pl.pallas_call(
        functools.partial(_pool_kernel, n_hist=n_hist, group_dim=group_dim),
        out_shape=jax.ShapeDtypeStruct((b, t, width), BF16),
        grid=(b, t // tm),
        in_specs=[pl.BlockSpec((1, tm, width), lambda bi, ti: (bi, ti, col_block)),
                  pl.BlockSpec((1, pad, width), lambda bi, ti: (bi, 0, 0)),
                  pl.BlockSpec((groups, group_dim, group_dim), lambda bi, ti: (0, 0, 0)),
                  pl.BlockSpec((1, width), lambda bi, ti: (0, 0))],
        out_specs=pl.BlockSpec((1, tm, width), lambda bi, ti: (bi, ti, 0)),
        scratch_shapes=[pltpu.VMEM((tm + pad, width), F32)],
        compiler_params=_params("parallel", "arbitrary"),
        name="pool",
    )(proj, hist_pad, w_pool, scale.reshape(1, width))


def _cumsum_rows(x):
    row = lax.broadcasted_iota(jnp.int32, x.shape, 0)
    shift = 1
    while shift < x.shape[0]:
        x = x + jnp.where(row >= shift, pltpu.roll(x, shift, 0), 0.0)
        shift *= 2
    return x


def _hgrn_kernel(q_ref, f_ref, i_ref, g_ref, la_ref, lc_ref, ng_ref, ones_ref, s0_ref, o_ref, sout_ref,
                 s_ref, ck_ref, vs_ref, *, n_heads, n_chunks):
    tb = pl.program_id(1)
    half = HG_CHUNK // 2
    hd = HEAD_DIM

    @pl.when(tb == 0)
    def _():
        s_ref[...] = s0_ref[0]

    la = la_ref[...]
    lc = lc_ref[...]
    ng = ng_ref[...]
    row8 = lax.broadcasted_iota(jnp.int32, (half, hd), 0)

    def chunk(c, carry):
        r0 = pl.multiple_of(c * HG_CHUNK, HG_CHUNK)
        qc = q_ref[0, pl.ds(r0, HG_CHUNK), :]
        fc = f_ref[0, pl.ds(r0, HG_CHUNK), :]
        gc = g_ref[0, pl.ds(r0, HG_CHUNK), :]
        q = qc * _sigmoid(qc)
        gate = gc * _sigmoid(gc)
        l1 = jnp.log(1.0 + jnp.exp(-jnp.abs(fc)))
        log_sig = jnp.minimum(fc, 0.0) - l1
        log_sig_neg = jnp.minimum(-fc, 0.0) - l1
        b = lc + log_sig
        log_f = jnp.maximum(la, b) + jnp.log(1.0 + jnp.exp(-jnp.abs(la - b)))
        log2_k = (lc + log_sig_neg) * LOG2_E
        cum = _cumsum_rows(log_f * LOG2_E)
        last = cum[HG_CHUNK - 1:HG_CHUNK, :]
        qt = (q * jnp.exp2(cum)).astype(BF16)
        kt = jnp.exp2(log2_k + last - cum).astype(BF16)
        dec = jnp.exp2(last)
        ck_ref[...] = cum - log2_k
        v = i_ref[0, pl.ds(r0, HG_CHUNK), :]
        vs_ref[...] = v
        vb = v.astype(BF16)
        for h in range(n_heads):
            sl = slice(h * hd, (h + 1) * hd)
            cum_a, cum_b = cum[0:half, sl], cum[half:, sl]
            q_a, q_b = q[0:half, sl], q[half:, sl]
            pairs = []
            for s in range(HG_CHUNK):
                cs = jnp.broadcast_to(ck_ref[s:s + 1, sl], (half, hd))
                if s < half:
                    pairs.append(q_a * jnp.where(row8 >= s, jnp.exp2(cum_a - cs), 0.0))
                    pairs.append(q_b * jnp.exp2(cum_b - cs))
                else:
                    pairs.append(q_b * jnp.where(row8 + half >= s, jnp.exp2(cum_b - cs), 0.0))
            scores = jnp.dot(jnp.concatenate(pairs, axis=0).astype(BF16), ones_ref[...],
                             preferred_element_type=F32)
            o_a = jnp.zeros((half, hd), F32)
            o_b = jnp.zeros((half, hd), F32)
            blocks = iter(range(len(pairs)))
            for s in range(HG_CHUNK):
                vs = jnp.broadcast_to(vs_ref[s:s + 1, sl], (half, hd))
                if s < half:
                    i = next(blocks)
                    o_a = o_a + scores[i * half:(i + 1) * half] * vs
                i = next(blocks)
                o_b = o_b + scores[i * half:(i + 1) * half] * vs
            st = s_ref[h]
            inter = lax.dot_general(qt[:, sl], st.astype(BF16), (((1,), (1,)), ((), ())),
                                    preferred_element_type=F32)
            o = jnp.concatenate([o_a, o_b], axis=0) + inter
            ms = jnp.mean(o * o, axis=-1, keepdims=True)
            o = o * lax.rsqrt(ms + EPS) * ng * gate[:, sl]
            o_ref[0, pl.ds(r0, HG_CHUNK), sl] = o.astype(o_ref.dtype)
            s_ref[h] = st * dec[:, sl] + lax.dot_general(vb[:, sl], kt[:, sl], (((0,), (0,)), ((), ())),
                                                         preferred_element_type=F32)
        return carry

    lax.fori_loop(0, n_chunks, chunk, 0, unroll=max(u for u in (1, 2, 3) if n_chunks % u == 0))

    @pl.when(tb == pl.num_programs(1) - 1)
    def _():
        sout_ref[0] = s_ref[...]


def _hgrn(proj, first_block, width, log_lb, log_1m_lb, norm_g, s0_t, tb_rows):
    b, t, _ = proj.shape
    n_heads = width // HEAD_DIM
    assert t % tb_rows == 0 and tb_rows % HG_CHUNK == 0
    col = lambda off: pl.BlockSpec((1, tb_rows, width), lambda bi, ti: (bi, ti, first_block + off))
    vec = pl.BlockSpec((1, width), lambda bi, ti: (0, 0))
    state_spec = pl.BlockSpec((1, n_heads, HEAD_DIM, HEAD_DIM), lambda bi, ti: (bi, 0, 0, 0))
    return pl.pallas_call(
        functools.partial(_hgrn_kernel, n_heads=n_heads, n_chunks=tb_rows // HG_CHUNK),
        out_shape=(jax.ShapeDtypeStruct((b, t, width), BF16),
                   jax.ShapeDtypeStruct((b, n_heads, HEAD_DIM, HEAD_DIM), F32)),
        grid=(b, t // tb_rows),
        in_specs=[col(0), col(1), col(2), col(3), vec, vec,
                  pl.BlockSpec((1, HEAD_DIM), lambda bi, ti: (0, 0)),
                  pl.BlockSpec((HEAD_DIM, HEAD_DIM), lambda bi, ti: (0, 0)), state_spec],
        out_specs=(pl.BlockSpec((1, tb_rows, width), lambda bi, ti: (bi, ti, 0)), state_spec),
        scratch_shapes=[pltpu.VMEM((n_heads, HEAD_DIM, HEAD_DIM), F32), pltpu.VMEM((HG_CHUNK, width), F32),
                        pltpu.VMEM((HG_CHUNK, width), F32)],
        compiler_params=_params("parallel", "arbitrary"),
        name="hgrn",
    )(proj, proj, proj, proj, log_lb.reshape(1, width), log_1m_lb.reshape(1, width),
      norm_g.reshape(1, HEAD_DIM).astype(F32), jnp.ones((HEAD_DIM, HEAD_DIM), BF16), s0_t)


def _layer(x, batch, seq, lw, layer, sb_cache, pool_hist, n_hist, s0, conv_hist, kv_stack=None):
    rows, d = x.shape
    sb_width = d // 4
    pool_width = d // 4
    hg_width = d // 2
    n_sb_heads = sb_width // HEAD_DIM
    tm = _row_tile(seq, MAX_ROW_TILE) if conv_hist is None else rows
    assert rows % tm == 0

    h = _rmsnorm(x, lw["norm1_g"], BF16)
    proj = _matmul(h, lw["w_in"], layer, tm)
    proj3 = proj.reshape(batch, seq, 3 * d)
    if sb_cache is None:
        a_out = _sb_prompt(proj3, n_sb_heads)
    else:
        a_out = _sb_sample(proj3, sb_cache[0], sb_cache[1], layer, n_sb_heads)
    seq_tile = _row_tile(seq, MAX_ROW_TILE)
    b_out = _pool(proj3, 3, pool_width, pool_hist, n_hist, lw["pool_w"], lw["pool_scale"], seq_tile)
    c_out, s_new_t = _hgrn(proj3, 2, hg_width, lw["log_lb"], lw["log_1m_lb"], lw["hgrn_norm_g"],
                           jnp.swapaxes(s0, -1, -2), _row_tile(seq, MAX_HGRN_ROWS))
    x = _outproj(a_out.reshape(rows, sb_width), b_out.reshape(rows, pool_width),
                 c_out.reshape(rows, hg_width), lw["w_out"], layer, x, tm)

    h2 = _rmsnorm(x, lw["norm2_g"], BF16)
    act, g_tail = _gateup(h2, lw["w_gate"], lw["w_up"], layer, lw["conv_w"], lw["conv_b"], tm, seq, conv_hist)
    x = _down(act, lw["w_down"], layer, x, tm)
    conv_state = g_tail[:, 8 - (CONV_W - 1):, :]

    if kv_stack is None:
        k_new = proj3[:, :, sb_width:2 * sb_width].reshape(batch, seq, n_sb_heads, HEAD_DIM)
        v_new = proj3[:, :, 2 * sb_width:3 * sb_width].reshape(batch, seq, n_sb_heads, HEAD_DIM)
    else:
        n_layers, k_stack, v_stack = kv_stack
        k_new = _to_heads(proj3, 1, n_sb_heads, k_stack, n_layers, layer, seq_tile)
        v_new = _to_heads(proj3, 2, n_sb_heads, v_stack, n_layers, layer, seq_tile)
    xb = proj3[:, :, 3 * sb_width:3 * sb_width + pool_width]
    pool_new = jnp.concatenate([pool_hist, xb], axis=1)[:, -POOL_HIST:]
    s_new = jnp.swapaxes(s_new_t, -1, -2)
    return x, (k_new, v_new, s_new, pool_new, conv_state)


def kernel(x_prompt, x_sample, cache_sb_k, cache_sb_v, state_hgrn, state_pool, state_conv, meta_tokens,
           norm1_g, w_in, pool_w, pool_scale, hgrn_lower_bounds, hgrn_norm_g, w_out, norm2_g, ffn_w_gate,
           ffn_w_up, ffn_conv_w, ffn_conv_b, ffn_w_down, final_norm_g):
    bp, seq_p, d = x_prompt.shape
    bs, seq_s, _ = x_sample.shape
    depth = w_in.shape[0]
    tp = N_META + seq_p
    hg_heads = (d // 2) // HEAD_DIM

    meta = jnp.broadcast_to(meta_tokens[None], (bp, N_META, d))
    xp = jnp.concatenate([meta, x_prompt], axis=1).reshape(bp * tp, d)
    xs = x_sample.reshape(bs * seq_s, d)

    probs = jax.nn.softmax(hgrn_lower_bounds.astype(F32), axis=0)
    lower = jnp.maximum(jnp.cumsum(probs, axis=0) - probs[0], 0.0)
    log_lb = jnp.log(lower)
    log_1m_lb = jnp.log1p(-lower)

    w_in_b, w_out_b = w_in.astype(BF16), w_out.astype(BF16)
    w_gate_b, w_up_b, w_down_b = ffn_w_gate.astype(BF16), ffn_w_up.astype(BF16), ffn_w_down.astype(BF16)
    pool_w_b = pool_w.astype(BF16)
    cache = (cache_sb_k.reshape(depth, bs, -1, d // 4), cache_sb_v.reshape(depth, bs, -1, d // 4))

    outs_p, outs_s = [], []
    pk = pv = None
    for l in range(depth):
        lw = dict(norm1_g=norm1_g[l], w_in=w_in_b, pool_w=pool_w_b[l], pool_scale=pool_scale[l],
                  log_lb=log_lb[l], log_1m_lb=log_1m_lb[l], hgrn_norm_g=hgrn_norm_g[l], w_out=w_out_b,
                  norm2_g=norm2_g[l], w_gate=w_gate_b, w_up=w_up_b, conv_w=ffn_conv_w[l], conv_b=ffn_conv_b[l],
                  w_down=w_down_b)
        xp, out = _layer(xp, bp, tp, lw, l, None, jnp.zeros((bp, POOL_HIST, d // 4), F32), 0,
                         jnp.zeros((bp, hg_heads, HEAD_DIM, HEAD_DIM), F32), None, (depth, pk, pv))
        pk, pv = out[0], out[1]
        outs_p.append(out)
        xs, out = _layer(xs, bs, seq_s, lw, l, cache, state_pool[l], POOL_HIST, state_hgrn[l], state_conv[l])
        outs_s.append(out)

    y_prompt = _rmsnorm_skip(xp.reshape(bp, tp, d), final_norm_g, N_META, F32)
    y_sample = _rmsnorm(xs, final_norm_g, F32).reshape(bs, seq_s, d)
    stack = lambda outs, idx: jnp.stack([o[idx] for o in outs])
    return (y_prompt, y_sample,
            pk, pv, stack(outs_p, 2), stack(outs_p, 3), stack(outs_p, 4),
            stack(outs_s, 0), stack(outs_s, 1), stack(outs_s, 2), stack(outs_s, 3), stack(outs_s, 4))
```
